```python
import functools
import jax, jax.numpy as jnp
from jax import lax
import numpy as np

D_MODEL = 1024
BATCH = 8
SEQ = 4096
DEPTH = 1
DEC_BATCH = 128
DEC_SEQ = 4
PAST_LEN = 16384
PAGE_SIZE = 128

HEAD_DIM = 64
N_HEADS = 8
N_KV_HEADS = 2
GROUP = N_HEADS // N_KV_HEADS
ATTN_DIM = N_HEADS * HEAD_DIM
KV_DIM = N_KV_HEADS * HEAD_DIM
WINDOW = 128
BLOCK = WINDOW
ROPE_THETA = 500000.0
ROPE_DIM = HEAD_DIM // 4
CONV_DIM = D_MODEL // 2
CONV_WIDTH = 31
FFN_DIM = -(-8 * D_MODEL // (3 * 256)) * 256
RMS_EPS = 1e-6
LN_EPS = 1e-5
NEG_INF = -1e30
IN_SIZES = (ATTN_DIM, KV_DIM, KV_DIM, CONV_DIM, CONV_DIM, D_MODEL, D_MODEL)
IN_COLS = sum(IN_SIZES)
IN_SPLITS = tuple(int(s) for s in np.cumsum(IN_SIZES)[:-1])

kernel_name = "hybrid_conformer_conv_swa_sink_decoder_step"


def rms_norm(x, g):
    xf = x.astype(jnp.float32)
    y = xf * lax.rsqrt(jnp.mean(xf * xf, axis=-1, keepdims=True) + RMS_EPS)
    return (y * g.astype(jnp.float32)).astype(x.dtype)


def adaln_mods(c, w_ada, b_ada):
    mods = jax.nn.silu(c) @ w_ada + b_ada
    return jnp.split(mods[:, None, :], 6, axis=-1)


def partial_rope(x, pos):
    half = ROPE_DIM // 2
    inv = ROPE_THETA ** (-jnp.arange(0, ROPE_DIM, 2, dtype=jnp.float32) / ROPE_DIM)
    ang = pos[:, None] * inv[None, :]
    cos = jnp.cos(ang)[:, None, :]
    sin = jnp.sin(ang)[:, None, :]
    xf = x.astype(jnp.float32)
    x1 = xf[..., :half]
    x2 = xf[..., half:ROPE_DIM]
    out = jnp.concatenate([x1 * cos - x2 * sin, x2 * cos + x1 * sin, xf[..., ROPE_DIM:]], axis=-1)
    return out.astype(x.dtype)


def sink_softmax(s, mask, sink):
    s = jnp.where(mask, s, NEG_INF)
    m = jnp.maximum(jnp.max(s, axis=-1, keepdims=True), sink)
    p = jnp.exp(s - m)
    den = jnp.sum(p, axis=-1, keepdims=True) + jnp.exp(sink - m)
    return p / den


def project_mixer_inputs(h, w_in, pos):
    B, T, _ = h.shape
    z = h @ w_in
    q, k, v, glu_a, glu_b, g_att, g_conv = jnp.split(z, IN_SPLITS, axis=-1)
    q = partial_rope(q.reshape(B, T, N_HEADS, HEAD_DIM), pos)
    k = partial_rope(k.reshape(B, T, N_KV_HEADS, HEAD_DIM), pos)
    v = v.reshape(B, T, N_KV_HEADS, HEAD_DIM)
    u = glu_a * jax.nn.sigmoid(glu_b)
    return q, k, v, u, g_att, g_conv


def swa_prompt(q, k, v, sinks):
    B, T = q.shape[:2]
    nb = T // BLOCK
    scale = HEAD_DIM ** -0.5
    qb = q.reshape(B, nb, BLOCK, N_KV_HEADS, GROUP, HEAD_DIM)
    pad = jnp.zeros((B, BLOCK, N_KV_HEADS, HEAD_DIM), k.dtype)
    kp = jnp.concatenate([pad, k], axis=1).reshape(B, nb + 1, BLOCK, N_KV_HEADS, HEAD_DIM)
    vp = jnp.concatenate([pad, v], axis=1).reshape(B, nb + 1, BLOCK, N_KV_HEADS, HEAD_DIM)
    kb = jnp.concatenate([kp[:, :-1], kp[:, 1:]], axis=2)
    vb = jnp.concatenate([vp[:, :-1], vp[:, 1:]], axis=2)
    s = jnp.einsum('bnqkgd,bnskd->bnkgqs', qb, kb, preferred_element_type=jnp.float32) * scale
    blk = jnp.arange(nb)[:, None] * BLOCK
    qpos = blk + jnp.arange(BLOCK)[None, :]
    kpos = blk - BLOCK + jnp.arange(2 * BLOCK)[None, :]
    rel = qpos[:, :, None] - kpos[:, None, :]
    mask = (rel >= 0) & (rel < WINDOW) & (kpos[:, None, :] >= 0)
    mask = mask[None, :, None, None]
    sink = sinks.astype(jnp.float32).reshape(1, 1, N_KV_HEADS, GROUP, 1, 1)
    probs = sink_softmax(s, mask, sink)
    o = jnp.einsum('bnkgqs,bnskd->bnqkgd', probs.astype(v.dtype), vb)
    return o.reshape(B, T, ATTN_DIM)


def swa_sample(q, k_new, v_new, k_buf, v_buf, sinks):
    Bd, S = q.shape[:2]
    L = k_buf.shape[1]
    scale = HEAD_DIM ** -0.5
    kc = jnp.concatenate([k_buf, k_new], axis=1)
    vc = jnp.concatenate([v_buf, v_new], axis=1)
    qg = q.reshape(Bd, S, N_KV_HEADS, GROUP, HEAD_DIM)
    s = jnp.einsum('btkgd,bskd->bkgts', qg, kc, preferred_element_type=jnp.float32) * scale
    qpos = PAST_LEN + jnp.arange(S)
    kpos = PAST_LEN - L + jnp.arange(L + S)
    rel = qpos[:, None] - kpos[None, :]
    mask = ((rel >= 0) & (rel < WINDOW))[None, None, None]
    sink = sinks.astype(jnp.float32).reshape(1, N_KV_HEADS, GROUP, 1, 1)
    probs = sink_softmax(s, mask, sink)
    o = jnp.einsum('bkgts,bskd->btkgd', probs.astype(vc.dtype), vc)
    return o.reshape(Bd, S, ATTN_DIM), kc[:, -L:], vc[:, -L:]


def conv_branch(u_hist, conv_w, conv_b, ln_g, ln_b, w_conv_o):
    dw = lax.conv_general_dilated(u_hist, conv_w[:, None, :], window_strides=(1,), padding='VALID',
                                  dimension_numbers=('NWC', 'WIO', 'NWC'),
                                  feature_group_count=CONV_DIM) + conv_b
    df = dw.astype(jnp.float32)
    mu = jnp.mean(df, axis=-1, keepdims=True)
    var = jnp.mean(jnp.square(df - mu), axis=-1, keepdims=True)
    n = ((df - mu) * lax.rsqrt(var + LN_EPS) * ln_g.astype(jnp.float32) + ln_b.astype(jnp.float32)).astype(dw.dtype)
    return jax.nn.silu(n) @ w_conv_o


def merge_branches(a, cv_out, g_att, g_conv, lw):
    a_out = a @ lw['w_attn_o']
    m = jax.nn.sigmoid(g_att) * a_out + jax.nn.sigmoid(g_conv) * cv_out
    return m @ lw['w_out']


def mixer_prompt(h, lw):
    B, T, _ = h.shape
    pos = jnp.arange(T, dtype=jnp.float32)
    q, k, v, u, g_att, g_conv = project_mixer_inputs(h, lw['w_in'], pos)
    a = swa_prompt(q, k, v, lw['sinks'])
    u_hist = jnp.concatenate([jnp.zeros((B, CONV_WIDTH - 1, CONV_DIM), u.dtype), u], axis=1)
    cv_out = conv_branch(u_hist, lw['conv_w'], lw['conv_b'], lw['conv_ln_g'], lw['conv_ln_b'], lw['w_conv_o'])
    out = merge_branches(a, cv_out, g_att, g_conv, lw)
    L = min(WINDOW, T)
    return out, (k[:, -L:], v[:, -L:], u[:, -(CONV_WIDTH - 1):])


def mixer_sample(h, k_buf, v_buf, conv_buf, lw):
    T = h.shape[1]
    pos = PAST_LEN + jnp.arange(T, dtype=jnp.float32)
    q, k, v, u, g_att, g_conv = project_mixer_inputs(h, lw['w_in'], pos)
    a, k_new_buf, v_new_buf = swa_sample(q, k, v, k_buf, v_buf, lw['sinks'])
    u_hist = jnp.concatenate([conv_buf, u], axis=1)
    cv_out = conv_branch(u_hist, lw['conv_w'], lw['conv_b'], lw['conv_ln_g'], lw['conv_ln_b'], lw['w_conv_o'])
    out = merge_branches(a, cv_out, g_att, g_conv, lw)
    return out, (k_new_buf, v_new_buf, u_hist[:, -(CONV_WIDTH - 1):])


def swiglu(h, w_ffn_in, w_ffn_out):
    g, u = jnp.split(h @ w_ffn_in, 2, axis=-1)
    return (jax.nn.silu(g) * u) @ w_ffn_out


def trunk_layer(x, c, lw, mixer):
    sh1, sc1, g1, sh2, sc2, g2 = adaln_mods(c, lw['w_ada'], lw['b_ada'])
    h = rms_norm(x, lw['norm1_g']) * (1 + sc1) + sh1
    m, states = mixer(h)
    x = x + g1 * m
    h2 = rms_norm(x, lw['norm2_g']) * (1 + sc2) + sh2
    x = x + g2 * swiglu(h2, lw['w_ffn_in'], lw['w_ffn_out'])
    return x, states


def setup_inputs(seed: int = 0) -> dict:
    key = jax.random.key(seed)
    ks = jax.random.split(key, 24)
    f32 = jnp.float32
    nrm = lambda k, shape, s: jax.random.normal(k, shape, f32) * s
    win_rows = min(WINDOW, PAST_LEN)
    return {
        'x_prompt': nrm(ks[0], (BATCH, SEQ, D_MODEL), 1.0),
        'x_sample': nrm(ks[1], (DEC_BATCH, DEC_SEQ, D_MODEL), 1.0),
        'state_k_win': nrm(ks[2], (DEPTH, DEC_BATCH, win_rows, N_KV_HEADS, HEAD_DIM), 1.0),
        'state_v_win': nrm(ks[3], (DEPTH, DEC_BATCH, win_rows, N_KV_HEADS, HEAD_DIM), 1.0),
        'state_conv': nrm(ks[4], (DEPTH, DEC_BATCH, CONV_WIDTH - 1, CONV_DIM), 0.5),
        'c_prompt': nrm(ks[5], (BATCH, D_MODEL), 1.0),
        'c_sample': nrm(ks[6], (DEC_BATCH, D_MODEL), 1.0),
        'norm1_g': 1.0 + nrm(ks[7], (DEPTH, D_MODEL), 0.01),
        'norm2_g': 1.0 + nrm(ks[8], (DEPTH, D_MODEL), 0.01),
        'w_ada': nrm(ks[9], (DEPTH, D_MODEL, 6 * D_MODEL), 0.5 * D_MODEL ** -0.5),
        'b_ada': nrm(ks[10], (DEPTH, 6 * D_MODEL), 0.01),
        'w_in': nrm(ks[11], (DEPTH, D_MODEL, IN_COLS), D_MODEL ** -0.5),
        'sinks': nrm(ks[12], (DEPTH, N_HEADS), 1.0),
        'w_attn_o': nrm(ks[13], (DEPTH, ATTN_DIM, D_MODEL), ATTN_DIM ** -0.5),
        'conv_w': nrm(ks[14], (DEPTH, CONV_WIDTH, CONV_DIM), CONV_WIDTH ** -0.5),
        'conv_b': nrm(ks[15], (DEPTH, CONV_DIM), 0.01),
        'conv_ln_g': 1.0 + nrm(ks[16], (DEPTH, CONV_DIM), 0.01),
        'conv_ln_b': nrm(ks[17], (DEPTH, CONV_DIM), 0.01),
        'w_conv_o': nrm(ks[18], (DEPTH, CONV_DIM, D_MODEL), CONV_DIM ** -0.5),
        'w_out': nrm(ks[19], (DEPTH, D_MODEL, D_MODEL), D_MODEL ** -0.5),
        'w_ffn_in': nrm(ks[20], (DEPTH, D_MODEL, 2 * FFN_DIM), D_MODEL ** -0.5),
        'w_ffn_out': nrm(ks[21], (DEPTH, FFN_DIM, D_MODEL), FFN_DIM ** -0.5),
        'final_norm_g': 1.0 + nrm(ks[22], (D_MODEL,), 0.01),
    }


def reference(x_prompt, x_sample, state_k_win, state_v_win, state_conv, c_prompt, c_sample,
              norm1_g, norm2_g, w_ada, b_ada, w_in, sinks, w_attn_o, conv_w, conv_b,
              conv_ln_g, conv_ln_b, w_conv_o, w_out, w_ffn_in, w_ffn_out, final_norm_g):
    xp, xs = x_prompt, x_sample
    kp_l, vp_l, cp_l, ks_l, vs_l, cs_l = [], [], [], [], [], []
    for l in range(DEPTH):
        lw = dict(norm1_g=norm1_g[l], norm2_g=norm2_g[l], w_ada=w_ada[l], b_ada=b_ada[l],
                  w_in=w_in[l], sinks=sinks[l], w_attn_o=w_attn_o[l], conv_w=conv_w[l],
                  conv_b=conv_b[l], conv_ln_g=conv_ln_g[l], conv_ln_b=conv_ln_b[l],
                  w_conv_o=w_conv_o[l], w_out=w_out[l], w_ffn_in=w_ffn_in[l], w_ffn_out=w_ffn_out[l])
        xp, (kp, vp, cp) = trunk_layer(xp, c_prompt, lw, functools.partial(mixer_prompt, lw=lw))
        xs, (ksn, vsn, csn) = trunk_layer(
            xs, c_sample, lw,
            functools.partial(mixer_sample, k_buf=state_k_win[l], v_buf=state_v_win[l],
                              conv_buf=state_conv[l], lw=lw))
        kp_l.append(kp); vp_l.append(vp); cp_l.append(cp)
        ks_l.append(ksn); vs_l.append(vsn); cs_l.append(csn)
    y_prompt = rms_norm(xp, final_norm_g)
    y_sample = rms_norm(xs, final_norm_g)
    k_win_prompt = jnp.stack(kp_l)
    v_win_prompt = jnp.stack(vp_l)
    conv_prompt = jnp.stack(cp_l)
    k_win_sample = jnp.stack(ks_l)
    v_win_sample = jnp.stack(vs_l)
    conv_sample = jnp.stack(cs_l)
    return (y_prompt, y_sample, k_win_prompt, v_win_prompt, conv_prompt, k_win_sample, v_win_sample, conv_sample)
```

```python
import functools

import jax
import jax.numpy as jnp
import numpy as np
from jax import lax
from jax.experimental import pallas as pl
from jax.experimental.pallas import tpu as pltpu

D_MODEL = 1024
HEAD_DIM = 64
N_HEADS = 8
N_KV_HEADS = 2
GROUP = N_HEADS // N_KV_HEADS
ATTN_DIM = N_HEADS * HEAD_DIM
KV_DIM = N_KV_HEADS * HEAD_DIM
WINDOW = 128
ROPE_THETA = 500000.0
ROPE_DIM = HEAD_DIM // 4
ROPE_HALF = ROPE_DIM // 2
CONV_DIM = D_MODEL // 2
CONV_WIDTH = 31
CONV_HIST = CONV_WIDTH - 1
FFN_DIM = -(-8 * D_MODEL // (3 * 256)) * 256
PAST_LEN = 16384
RMS_EPS = 1e-6
LN_EPS = 1e-5
NEG_INF = -1e30

LANES = 128
HIST_PAD = 32
QK_COLS = ATTN_DIM + KV_DIM
QKV_COLS = ATTN_DIM + 2 * KV_DIM
VMEM_LIMIT = 56 * 1024 * 1024

_F32 = jnp.float32
_BF16 = jnp.bfloat16


def _dot(a, b):
    return jnp.dot(a, b, preferred_element_type=_F32)


def _dot_nt(a, b):
    return lax.dot_general(a, b, (((1,), (1,)), ((), ())), preferred_element_type=_F32)


def _rms(x, g):
    return x * lax.rsqrt(jnp.mean(x * x, axis=-1, keepdims=True) + RMS_EPS) * g


def _sigmoid(x):
    return 1.0 / (1.0 + jnp.exp(-x))


def _const_spec(shape):
    nd = len(shape)
    return pl.BlockSpec(shape, lambda *_: (0,) * nd, pipeline_mode=pl.Buffered(1))


def _mods_kernel(c_ref, w_ref, b_ref, o_ref):
    c = c_ref[...]
    a = (c * _sigmoid(c)).astype(_BF16)
    o_ref[...] = _dot(a, w_ref[...].astype(_BF16)) + b_ref[...]


def _adaln_mods(c, w_ada, b_ada):
    n = c.shape[0]
    cols = w_ada.shape[1]
    bn = D_MODEL
    return pl.pallas_call(
        _mods_kernel,
        grid=(cols // bn,),
        in_specs=[
            pl.BlockSpec((n, D_MODEL), lambda j: (0, 0)),
            pl.BlockSpec((D_MODEL, bn), lambda j: (0, j)),
            pl.BlockSpec((1, bn), lambda j: (0, j)),
        ],
        out_specs=pl.BlockSpec((n, bn), lambda j: (0, j)),
        out_shape=jax.ShapeDtypeStruct((n, cols), _F32),
        compiler_params=pltpu.CompilerParams(dimension_semantics=("arbitrary",)),
        name="adaln_mods",
    )(c, w_ada, b_ada.reshape(1, cols))


def _rope_tables(pos, inv_lane):
    ang = pos * inv_lane
    cos = jnp.cos(ang)
    sin = jnp.sin(ang)
    lane = lax.broadcasted_iota(jnp.int32, (1, LANES), 1) % HEAD_DIM
    c = jnp.where(lane < ROPE_DIM, cos, 1.0)
    s_up = jnp.where((lane >= ROPE_HALF) & (lane < ROPE_DIM), sin, 0.0)
    s_dn = jnp.where(lane < ROPE_HALF, -sin, 0.0)
    return c, s_up, s_dn


def _apply_rope(z, c, s_up, s_dn):
    outs = []
    for j in range(z.shape[1] // LANES):
        zz = z[:, j * LANES:(j + 1) * LANES]
        outs.append(zz * c + pltpu.roll(zz, ROPE_HALF, 1) * s_up
                    + pltpu.roll(zz, LANES - ROPE_HALF, 1) * s_dn)
    return jnp.concatenate(outs, axis=1)


def _front_body(x, sh, sc, n1g, wqkv_ref, wglu_ref, wgate_ref, tabs):
    h = (_rms(x, n1g) * (1.0 + sc) + sh).astype(_BF16)
    zqkv = _dot(h, wqkv_ref[...])
    qk = _apply_rope(zqkv[:, :QK_COLS], *tabs)
    q = (qk[:, :ATTN_DIM] * (HEAD_DIM ** -0.5)).astype(_BF16)
    k = qk[:, ATTN_DIM:]
    v = zqkv[:, QK_COLS:]
    zg = _dot(h, wglu_ref[...])
    u = zg[:, :CONV_DIM] * _sigmoid(zg[:, CONV_DIM:])
    gates = _sigmoid(_dot(h, wgate_ref[...])).astype(_BF16)
    return q, k, v, u, gates


def _front_prompt_kernel(x_ref, mods_ref, n1g_ref, inv_ref, wqkv_ref, wglu_ref, wgate_ref,
                         qkv_ref, u_ref, gates_ref, kvlast_ref, tab_ref):
    t = pl.program_id(0)
    b = pl.program_id(1)
    tt = x_ref.shape[1]

    @pl.when(b == 0)
    def _():
        pos = (t * tt + lax.broadcasted_iota(jnp.int32, (tt, 1), 0)).astype(_F32)
        c, s_up, s_dn = _rope_tables(pos, inv_ref[...])
        tab_ref[0] = c
        tab_ref[1] = s_up
        tab_ref[2] = s_dn

    sh = mods_ref[0, 0:1, :]
    sc = mods_ref[0, 1:2, :]
    q, k, v, u, gates = _front_body(x_ref[0], sh, sc, n1g_ref[...], wqkv_ref, wglu_ref, wgate_ref,
                                    (tab_ref[0], tab_ref[1], tab_ref[2]))
    qkv_ref[0] = jnp.concatenate([q, k.astype(_BF16), v.astype(_BF16)], axis=1)
    u_ref[0] = u
    gates_ref[0] = gates
    kvlast_ref[0, 0] = jnp.concatenate([k[tt - WINDOW:], v[tt - WINDOW:]], axis=1)


def _front_prompt(x, mods, n1g, inv_lane, wqkv, wglu, wgate, tt):
    B, T, _ = x.shape
    grid = (T // tt, B)
    tok = lambda w: pl.BlockSpec((1, tt, w), lambda t, b: (b, t, 0))
    return pl.pallas_call(
        _front_prompt_kernel,
        grid=grid,
        in_specs=[
            tok(D_MODEL),
            pl.BlockSpec((1, 6, D_MODEL), lambda t, b: (b, 0, 0)),
            _const_spec((1, D_MODEL)),
            _const_spec((1, LANES)),
            _const_spec(wqkv.shape),
            _const_spec(wglu.shape),
            _const_spec(wgate.shape),
        ],
        out_specs=[
            tok(QKV_COLS),
            tok(CONV_DIM),
            tok(2 * D_MODEL),
            pl.BlockSpec((1, 1, WINDOW, 2 * KV_DIM), lambda t, b: (t, b, 0, 0)),
        ],
        out_shape=[
            jax.ShapeDtypeStruct((B, T, QKV_COLS), _BF16),
            jax.ShapeDtypeStruct((B, T, CONV_DIM), _F32),
            jax.ShapeDtypeStruct((B, T, 2 * D_MODEL), _BF16),
            jax.ShapeDtypeStruct((T // tt, B, WINDOW, 2 * KV_DIM), _F32),
        ],
        scratch_shapes=[pltpu.VMEM((3, tt, LANES), _F32)],
        compiler_params=pltpu.CompilerParams(
            dimension_semantics=("arbitrary", "arbitrary"), vmem_limit_bytes=VMEM_LIMIT),
        name="front_prompt",
    )(x, mods, n1g, inv_lane, wqkv, wglu, wgate)


def _conv_post(dw, lng, lnb):
    mu = jnp.mean(dw, axis=-1, keepdims=True)
    d = dw - mu
    var = jnp.mean(d * d, axis=-1, keepdims=True)
    n = d * lax.rsqrt(var + LN_EPS) * lng + lnb
    return (n * _sigmoid(n)).astype(_BF16)


def _ffn_chunks():
    step = 4 * 256
    return [(s, min(step, FFN_DIM - s)) for s in range(0, FFN_DIM, step)]


def _tail_body(x, a, s, gates, g1, sh2, sc2, g2, n2g, fng, wao_ref, wco_ref, wout_ref, wfi_ref, wfo_ref):
    a_out = _dot(a, wao_ref[...])
    cv_out = _dot(s, wco_ref[...])
    m = (gates[:, :D_MODEL].astype(_F32) * a_out + gates[:, D_MODEL:].astype(_F32) * cv_out).astype(_BF16)
    x1 = x + g1 * _dot(m, wout_ref[...])
    h2 = (_rms(x1, n2g) * (1.0 + sc2) + sh2).astype(_BF16)
    acc = None
    for (c0, cw) in _ffn_chunks():
        g = _dot(h2, wfi_ref[:, c0:c0 + cw])
        uu = _dot(h2, wfi_ref[:, FFN_DIM + c0:FFN_DIM + c0 + cw])
        act = (g * _sigmoid(g) * uu).astype(_BF16)
        part = _dot(act, wfo_ref[c0:c0 + cw, :])
        acc = part if acc is None else acc + part
    x2 = x1 + g2 * acc
    return _rms(x2, fng)


def _softmax_sink_pv(s, mask, sink, vg):
    s = jnp.where(mask, s, NEG_INF)
    m = jnp.maximum(jnp.max(s, axis=-1, keepdims=True), sink)
    p = jnp.exp(s - m)
    den = jnp.sum(p, axis=-1, keepdims=True) + jnp.exp(sink - m)
    return _dot(p.astype(_BF16), vg) / den


def _back_prompt_kernel(sinks_ref, x_ref, qkv_ref, u_ref, gates_ref, mods_ref,
                        cw_ref, cb_ref, lng_ref, lnb_ref, n2g_ref, fng_ref,
                        wao_ref, wco_ref, wout_ref, wfi_ref, wfo_ref,
                        y_ref, kvc_ref, hist_ref):
    t = pl.program_id(1)
    tt = x_ref.shape[1]
    nblk = tt // WINDOW

    @pl.when(t == 0)
    def _():
        kvc_ref[...] = jnp.zeros_like(kvc_ref)
        hist_ref[0:HIST_PAD, :] = jnp.zeros((HIST_PAD, CONV_DIM), _F32)

    hist_ref[HIST_PAD:HIST_PAD + tt, :] = u_ref[0]
    off = HIST_PAD - CONV_HIST
    dw = jnp.broadcast_to(cb_ref[...], (tt, CONV_DIM))
    for k in range(CONV_WIDTH):
        dw = dw + hist_ref[off + k:off + k + tt, :] * cw_ref[k:k + 1, :]
    s_act = _conv_post(dw, lng_ref[...], lnb_ref[...])
    hist_ref[0:HIST_PAD, :] = hist_ref[tt:tt + HIST_PAD, :]

    rows = lax.broadcasted_iota(jnp.int32, (GROUP * WINDOW, 2 * WINDOW), 0) % WINDOW
    cols = lax.broadcasted_iota(jnp.int32, (GROUP * WINDOW, 2 * WINDOW), 1)
    band = (cols > rows) & (cols <= rows + WINDOW)
    first_lo = WINDOW * (1 - jnp.minimum(t, 1))
    band_first = band & (cols >= first_lo)
    sink_cols = []
    for g in range(N_KV_HEADS):
        sink_cols.append(jnp.concatenate(
            [jnp.full((WINDOW, 1), sinks_ref[g * GROUP + h], _F32) for h in range(GROUP)], axis=0))

    kv_prev = kvc_ref[...].astype(_BF16)
    a_blocks = []
    for j in range(nblk):
        r0 = j * WINDOW
        qb = qkv_ref[0, r0:r0 + WINDOW, 0:ATTN_DIM]
        kv_cur = qkv_ref[0, r0:r0 + WINDOW, ATTN_DIM:QKV_COLS]
        kv = jnp.concatenate([kv_prev, kv_cur], axis=0)
        mask = band_first if j == 0 else band
        heads = []
        for g in range(N_KV_HEADS):
            qg = jnp.concatenate(
                [qb[:, (g * GROUP + h) * HEAD_DIM:(g * GROUP + h + 1) * HEAD_DIM] for h in range(GROUP)],
                axis=0)
            kg = kv[:, g * HEAD_DIM:(g + 1) * HEAD_DIM]
            vg = kv[:, KV_DIM + g * HEAD_DIM:KV_DIM + (g + 1) * HEAD_DIM]
            o = _softmax_sink_pv(_dot_nt(qg, kg), mask, sink_cols[g], vg)
            heads += [o[h * WINDOW:(h + 1) * WINDOW] for h in range(GROUP)]
        a_blocks.append(jnp.concatenate(heads, axis=1))
        kv_prev = kv_cur
    kvc_ref[...] = kv_prev.astype(_F32)
    a = jnp.concatenate(a_blocks, axis=0).astype(_BF16)

    md = lambda i: mods_ref[0, i:i + 1, :]
    y_ref[0] = _tail_body(x_ref[0], a, s_act, gates_ref[0], md(2), md(3), md(4), md(5),
                          n2g_ref[...], fng_ref[...], wao_ref, wco_ref, wout_ref, wfi_ref, wfo_ref)


def _back_prompt(sinks, x, qkv, u, gates, mods, cw, cb, lng, lnb, n2g, fng, wao, wco, wout, wfi, wfo, tt):
    B, T, _ = x.shape
    grid = (B, T // tt)
    tok = lambda w: pl.BlockSpec((1, tt, w), lambda b, t: (b, t, 0))
    consts = [cw, cb, lng, lnb, n2g, fng, wao, wco, wout, wfi, wfo]
    return pl.pallas_call(
        _back_prompt_kernel,
        grid=grid,
        in_specs=[
            pl.BlockSpec(memory_space=pltpu.SMEM),
            tok(D_MODEL), tok(QKV_COLS), tok(CONV_DIM), tok(2 * D_MODEL),
            pl.BlockSpec((1, 6, D_MODEL), lambda b, t: (b, 0, 0)),
        ] + [_const_spec(c.shape) for c in consts],
        out_specs=tok(D_MODEL),
        out_shape=jax.ShapeDtypeStruct((B, T, D_MODEL), _F32),
        scratch_shapes=[
            pltpu.VMEM((WINDOW, 2 * KV_DIM), _F32),
            pltpu.VMEM((HIST_PAD + tt, CONV_DIM), _F32),
        ],
        compiler_params=pltpu.CompilerParams(
            dimension_semantics=("arbitrary", "arbitrary"), vmem_limit_bytes=VMEM_LIMIT),
        name="back_prompt",
    )(sinks, x, qkv, u, gates, mods, *consts)


def _front_sample_kernel(x_ref, mods_ref, n1g_ref, inv_ref, wqkv_ref, wglu_ref, wgate_ref,
                         q_ref, kv_ref, u_ref, gates_ref):
    S = x_ref.shape[0]
    nb = x_ref.shape[1]
    pos = (PAST_LEN + lax.broadcasted_iota(jnp.int32, (8, 1), 0)).astype(_F32)
    c8, up8, dn8 = _rope_tables(pos, inv_ref[...])
    expand = lambda tab: jnp.concatenate(
        [jnp.broadcast_to(tab[s:s + 1, :], (nb, LANES)) for s in range(S)], axis=0)
    tabs = (expand(c8), expand(up8), expand(dn8))
    x = x_ref[...].reshape(S * nb, D_MODEL)
    sh = jnp.concatenate([mods_ref[0]] * S, axis=0)
    sc = jnp.concatenate([mods_ref[1]] * S, axis=0)
    q, k, v, u, gates = _front_body(x, sh, sc, n1g_ref[...], wqkv_ref, wglu_ref, wgate_ref, tabs)
    q_ref[...] = q.reshape(S, nb, ATTN_DIM)
    kv_ref[...] = jnp.concatenate([k, v], axis=1).reshape(S, nb, 2 * KV_DIM)
    u_ref[...] = u.reshape(S, nb, CONV_DIM)
    gates_ref[...] = gates.reshape(S, nb, 2 * D_MODEL)


def _front_sample(x, mods, n1g, inv_lane, wqkv, wglu, wgate):
    S, nb, _ = x.shape
    full = lambda shape: pl.BlockSpec(shape, lambda i: (0,) * len(shape))
    args = (x, mods, n1g, inv_lane, wqkv, wglu, wgate)
    return pl.pallas_call(
        _front_sample_kernel,
        grid=(1,),
        in_specs=[full(a.shape) for a in args],
        out_specs=[full((S, nb, ATTN_DIM)), full((S, nb, 2 * KV_DIM)), full((S, nb, CONV_DIM)),
                   full((S, nb, 2 * D_MODEL))],
        out_shape=[
            jax.ShapeDtypeStruct((S, nb, ATTN_DIM), _BF16),
            jax.ShapeDtypeStruct((S, nb, 2 * KV_DIM), _F32),
            jax.ShapeDtypeStruct((S, nb, CONV_DIM), _F32),
            jax.ShapeDtypeStruct((S, nb, 2 * D_MODEL), _BF16),
        ],
        compiler_params=pltpu.CompilerParams(
            dimension_semantics=("arbitrary",), vmem_limit_bytes=VMEM_LIMIT),
        name="front_sample",
    )(*args)


def _attn_sample_kernel(sinks_ref, q_ref, kvn_ref, kbuf_ref, vbuf_ref, a_ref, kout_ref, vout_ref):
    S = q_ref.shape[0]
    bc = q_ref.shape[1]
    L = kbuf_ref.shape[1]
    npad = 8
    rows = lax.broadcasted_iota(jnp.int32, (GROUP * S, L + npad), 0) % S
    cols = lax.broadcasted_iota(jnp.int32, (GROUP * S, L + npad), 1)
    mask = (cols > rows) & (cols <= rows + WINDOW) & (cols < L + S)
    sink_cols = []
    for g in range(N_KV_HEADS):
        sink_cols.append(jnp.concatenate(
            [jnp.full((S, 1), sinks_ref[g * GROUP + h], _F32) for h in range(GROUP)], axis=0))
    zpad = jnp.zeros((npad - S, 2 * KV_DIM), _F32)
    for b in range(bc):
        kvn = jnp.concatenate([kvn_ref[:, b, :], zpad], axis=0)
        kold = kbuf_ref[b]
        vold = vbuf_ref[b]
        kc = jnp.concatenate([kold, kvn[:, :KV_DIM]], axis=0)
        vc = jnp.concatenate([vold, kvn[:, KV_DIM:]], axis=0)
        kout_ref[b] = kc[S:S + L]
        vout_ref[b] = vc[S:S + L]
        kcb = kc.astype(_BF16)
        vcb = vc.astype(_BF16)
        qb = q_ref[:, b, :]
        heads = []
        for g in range(N_KV_HEADS):
            qg = jnp.concatenate(
                [qb[:, (g * GROUP + h) * HEAD_DIM:(g * GROUP + h + 1) * HEAD_DIM] for h in range(GROUP)],
                axis=0)
            kg = kcb[:, g * HEAD_DIM:(g + 1) * HEAD_DIM]
            vg = vcb[:, g * HEAD_DIM:(g + 1) * HEAD_DIM]
            o = _softmax_sink_pv(_dot_nt(qg, kg), mask, sink_cols[g], vg)
            heads += [o[h * S:(h + 1) * S] for h in range(GROUP)]
        a_ref[:, b, :] = jnp.concatenate(heads, axis=1).astype(_BF16)


def _attn_sample(sinks, q, kvn, kbuf, vbuf, bc):
    S, nb, _ = q.shape
    L = kbuf.shape[1]
    stepb = lambda w: pl.BlockSpec((S, bc, w), lambda i: (0, i, 0))
    buf = pl.BlockSpec((bc, L, KV_DIM), lambda i: (i, 0, 0))
    return pl.pallas_call(
        _attn_sample_kernel,
        grid=(nb // bc,),
        in_specs=[pl.BlockSpec(memory_space=pltpu.SMEM), stepb(ATTN_DIM), stepb(2 * KV_DIM), buf, buf],
        out_specs=[stepb(ATTN_DIM), buf, buf],
        out_shape=[
            jax.ShapeDtypeStruct((S, nb, ATTN_DIM), _BF16),
            jax.ShapeDtypeStruct((nb, L, KV_DIM), _F32),
            jax.ShapeDtypeStruct((nb, L, KV_DIM), _F32),
        ],
        compiler_params=pltpu.CompilerParams(dimension_semantics=("arbitrary",)),
        name="attn_sample",
    )(sinks, q, kvn, kbuf, vbuf)


def _back_sample_kernel(x_ref, a_ref, u_ref, gates_ref, mods_ref, hist_ref,
                        cw_ref, cb_ref, lng_ref, lnb_ref, n2g_ref, fng_ref,
                        wao_ref, wco_ref, wout_ref, wfi_ref, wfo_ref,
                        y_ref, hout_ref):
    S = x_ref.shape[0]
    nb = x_ref.shape[1]
    slab = lambda j: hist_ref[j] if j < CONV_HIST else u_ref[j - CONV_HIST]
    outs = []
    for s in range(S):
        dw = jnp.broadcast_to(cb_ref[...], (nb, CONV_DIM))
        for k in range(CONV_WIDTH):
            dw = dw + slab(s + k) * cw_ref[k:k + 1, :]
        outs.append(dw)
    for j in range(CONV_HIST):
        hout_ref[j] = slab(j + S)
    s_act = _conv_post(jnp.concatenate(outs, axis=0), lng_ref[...], lnb_ref[...])

    md = lambda i: jnp.concatenate([mods_ref[i]] * S, axis=0)
    y = _tail_body(x_ref[...].reshape(S * nb, D_MODEL), a_ref[...].reshape(S * nb, ATTN_DIM), s_act,
                   gates_ref[...].reshape(S * nb, 2 * D_MODEL), md(2), md(3), md(4), md(5),
                   n2g_ref[...], fng_ref[...], wao_ref, wco_ref, wout_ref, wfi_ref, wfo_ref)
    y_ref[...] = y.reshape(S, nb, D_MODEL)


def _back_sample(x, a, u, gates, mods, hist, cw, cb, lng, lnb, n2g, fng, wao, wco, wout, wfi, wfo):
    S, nb, _ = x.shape
    full = lambda shape: pl.BlockSpec(shape, lambda i: (0,) * len(shape))
    args = (x, a, u, gates, mods, hist, cw, cb, lng, lnb, n2g, fng, wao, wco, wout, wfi, wfo)
    return pl.pallas_call(
        _back_sample_kernel,
        grid=(1,),
        in_specs=[full(v.shape) for v in args],
        out_specs=[full((S, nb, D_MODEL)), full((CONV_HIST, nb, CONV_DIM))],
        out_shape=[
            jax.ShapeDtypeStruct((S, nb, D_MODEL), _F32),
            jax.ShapeDtypeStruct((CONV_HIST, nb, CONV_DIM), _F32),
        ],
        compiler_params=pltpu.CompilerParams(
            dimension_semantics=("arbitrary",), vmem_limit_bytes=VMEM_LIMIT),
        name="back_sample",
    )(*args)


def _inv_lane():
    inv = ROPE_THETA ** (-jnp.arange(0, ROPE_DIM, 2, dtype=_F32) / ROPE_DIM)
    return jnp.tile(inv, LANES // ROPE_HALF).reshape(1, LANES)


def _layer(xp, xs_t, kbuf, vbuf, cbuf_t, mods_p, mods_s_t, lw, final_g, tt_front, tt_back, bc):
    row = lambda v: v.reshape(1, -1)
    w_in = lw['w_in'].astype(_BF16)
    wqkv = w_in[:, :QKV_COLS]
    wglu = w_in[:, QKV_COLS:QKV_COLS + 2 * CONV_DIM]
    wgate = w_in[:, QKV_COLS + 2 * CONV_DIM:]
    wao = lw['w_attn_o'].astype(_BF16)
    wco = lw['w_conv_o'].astype(_BF16)
    wout = lw['w_out'].astype(_BF16)
    wfi = lw['w_ffn_in'].astype(_BF16)
    wfo = lw['w_ffn_out'].astype(_BF16)
    inv_lane = _inv_lane()
    n1g, n2g, fng = row(lw['norm1_g']), row(lw['norm2_g']), row(final_g)
    cw, cb, lng, lnb = lw['conv_w'], row(lw['conv_b']), row(lw['conv_ln_g']), row(lw['conv_ln_b'])
    sinks = lw['sinks']

    qkv, u, gates, kvlast = _front_prompt(xp, mods_p, n1g, inv_lane, wqkv, wglu, wgate, tt_front)
    yp = _back_prompt(sinks, xp, qkv, u, gates, mods_p, cw, cb, lng, lnb, n2g, fng,
                      wao, wco, wout, wfi, wfo, tt_back)

    q_s, kvn_s, u_s, gates_s = _front_sample(xs_t, mods_s_t, n1g, inv_lane, wqkv, wglu, wgate)
    a_s, kout, vout = _attn_sample(sinks, q_s, kvn_s, kbuf, vbuf, bc)
    ys_t, hout_t = _back_sample(xs_t, a_s, u_s, gates_s, mods_s_t, cbuf_t, cw, cb, lng, lnb, n2g, fng,
                                wao, wco, wout, wfi, wfo)
    return yp, ys_t, kvlast, u, kout, vout, hout_t


def kernel(x_prompt, x_sample, state_k_win, state_v_win, state_conv, c_prompt, c_sample, norm1_g, norm2_g,
           w_ada, b_ada, w_in, sinks, w_attn_o, conv_w, conv_b, conv_ln_g, conv_ln_b, w_conv_o, w_out,
           w_ffn_in, w_ffn_out, final_norm_g):
    B, T, _ = x_prompt.shape
    nb, S, _ = x_sample.shape
    depth = w_in.shape[0]
    assert depth == 1, "final norm is fused into the single trunk layer"
    L = state_k_win.shape[2]
    l = 0
    lw = dict(norm1_g=norm1_g[l], norm2_g=norm2_g[l], w_in=w_in[l], sinks=sinks[l], w_attn_o=w_attn_o[l],
              conv_w=conv_w[l], conv_b=conv_b[l], conv_ln_g=conv_ln_g[l], conv_ln_b=conv_ln_b[l],
              w_conv_o=w_conv_o[l], w_out=w_out[l], w_ffn_in=w_ffn_in[l], w_ffn_out=w_ffn_out[l])

    mods = _adaln_mods(jnp.concatenate([c_prompt, c_sample], axis=0), w_ada[l], b_ada[l])
    mods = mods.reshape(B + nb, 6, D_MODEL)
    mods_p = mods[:B]
    mods_s_t = mods[B:].transpose(1, 0, 2)

    xs_t = x_sample.transpose(1, 0, 2)
    kbuf = state_k_win[l].reshape(nb, L, KV_DIM)
    vbuf = state_v_win[l].reshape(nb, L, KV_DIM)
    cbuf_t = state_conv[l].transpose(1, 0, 2)

    tt_front = min(512, T)
    tt_back = min(256, T)
    yp, ys_t, kvtiles, u, kout, vout, hout_t = _layer(
        x_prompt, xs_t, kbuf, vbuf, cbuf_t, mods_p, mods_s_t, lw, final_norm_g, tt_front, tt_back, bc=8)

    y_sample = ys_t.transpose(1, 0, 2)
    kvlast = kvtiles[-1]
    k_win_prompt = kvlast[:, :, :KV_DIM].reshape(1, B, WINDOW, N_KV_HEADS, HEAD_DIM)
    v_win_prompt = kvlast[:, :, KV_DIM:].reshape(1, B, WINDOW, N_KV_HEADS, HEAD_DIM)
    conv_prompt = u[:, T - CONV_HIST:, :][None]
    k_win_sample = kout.reshape(1, nb, L, N_KV_HEADS, HEAD_DIM)
    v_win_sample = vout.reshape(1, nb, L, N_KV_HEADS, HEAD_DIM)
    conv_sample = hout_t.transpose(1, 0, 2)[None]
    return (yp, y_sample, k_win_prompt, v_win_prompt, conv_prompt, k_win_sample, v_win_sample, conv_sample)
```

```python
import functools

import jax
import jax.numpy as jnp
import numpy as np
from jax import lax
from jax.experimental import pallas as pl
from jax.experimental.pallas import tpu as pltpu

D_MODEL = 1024
HEAD_DIM = 64
N_HEADS = 8
N_KV_HEADS = 2
GROUP = N_HEADS // N_KV_HEADS
ATTN_DIM = N_HEADS * HEAD_DIM
KV_DIM = N_KV_HEADS * HEAD_DIM
WINDOW = 128
ROPE_THETA = 500000.0
ROPE_DIM = HEAD_DIM // 4
ROPE_HALF = ROPE_DIM // 2
CONV_DIM = D_MODEL // 2
CONV_WIDTH = 31
CONV_HIST = CONV_WIDTH - 1
FFN_DIM = -(-8 * D_MODEL // (3 * 256)) * 256
PAST_LEN = 16384
RMS_EPS = 1e-6
LN_EPS = 1e-5
NEG_INF = -1e30

LANES = 128
SUBLANES = 8
HIST_PAD = 32
QK_COLS = ATTN_DIM + KV_DIM
QKV_COLS = ATTN_DIM + 2 * KV_DIM
VMEM_LIMIT = 56 * 1024 * 1024

_F32 = jnp.float32
_BF16 = jnp.bfloat16


def _dot(a, b):
    return jnp.dot(a, b, preferred_element_type=_F32)


def _dot_nt(a, b):
    return lax.dot_general(a, b, (((1,), (1,)), ((), ())), preferred_element_type=_F32)


def _rms(x, g):
    return x * lax.rsqrt(jnp.mean(x * x, axis=-1, keepdims=True) + RMS_EPS) * g


def _sigmoid(x):
    return 1.0 / (1.0 + jnp.exp(-x))


def _const_spec(shape):
    nd = len(shape)
    return pl.BlockSpec(shape, lambda *_: (0,) * nd, pipeline_mode=pl.Buffered(1))


def _mods_kernel(c_ref, w_ref, b_ref, o_ref):
    c = c_ref[...]
    a = (c * _sigmoid(c)).astype(_BF16)
    o_ref[...] = _dot(a, w_ref[...].astype(_BF16)) + b_ref[...]


def _adaln_mods(c, w_ada, b_ada):
    n = c.shape[0]
    cols = w_ada.shape[1]
    bn = D_MODEL
    return pl.pallas_call(
        _mods_kernel,
        grid=(cols // bn,),
        in_specs=[
            pl.BlockSpec((n, D_MODEL), lambda j: (0, 0)),
            pl.BlockSpec((D_MODEL, bn), lambda j: (0, j)),
            pl.BlockSpec((1, bn), lambda j: (0, j)),
        ],
        out_specs=pl.BlockSpec((n, bn), lambda j: (0, j)),
        out_shape=jax.ShapeDtypeStruct((n, cols), _F32),
        compiler_params=pltpu.CompilerParams(dimension_semantics=("arbitrary",)),
        name="adaln_mods",
    )(c, w_ada, b_ada.reshape(1, cols))


def _rope_tables(pos, inv_lane):
    ang = pos * inv_lane
    cos = jnp.cos(ang)
    sin = jnp.sin(ang)
    lane = lax.broadcasted_iota(jnp.int32, (1, LANES), 1) % HEAD_DIM
    c = jnp.where(lane < ROPE_DIM, cos, 1.0)
    s_up = jnp.where((lane >= ROPE_HALF) & (lane < ROPE_DIM), sin, 0.0)
    s_dn = jnp.where(lane < ROPE_HALF, -sin, 0.0)
    return c, s_up, s_dn


def _apply_rope(z, c, s_up, s_dn):
    outs = []
    for j in range(z.shape[1] // LANES):
        zz = z[:, j * LANES:(j + 1) * LANES]
        outs.append(zz * c + pltpu.roll(zz, ROPE_HALF, 1) * s_up
                    + pltpu.roll(zz, LANES - ROPE_HALF, 1) * s_dn)
    return jnp.concatenate(outs, axis=1)


def _front_body(x, sh, sc, n1g, wqkv_ref, wglu_ref, wgate_ref, tabs):
    h = (_rms(x, n1g) * (1.0 + sc) + sh).astype(_BF16)
    zqkv = _dot(h, wqkv_ref[...])
    qk = _apply_rope(zqkv[:, :QK_COLS], *tabs)
    q = (qk[:, :ATTN_DIM] * (HEAD_DIM ** -0.5)).astype(_BF16)
    k = qk[:, ATTN_DIM:]
    v = zqkv[:, QK_COLS:]
    zg = _dot(h, wglu_ref[...])
    u = zg[:, :CONV_DIM] * _sigmoid(zg[:, CONV_DIM:])
    gates = _sigmoid(_dot(h, wgate_ref[...])).astype(_BF16)
    return q, k, v, u, gates


def _front_prompt_kernel(x_ref, mods_ref, n1g_ref, inv_ref, wqkv_ref, wglu_ref, wgate_ref,
                         qkv_ref, u_ref, gates_ref, kvlast_ref, tab_ref):
    t = pl.program_id(0)
    b = pl.program_id(1)
    tt = x_ref.shape[1]

    @pl.when(b == 0)
    def _():
        pos = (t * tt + lax.broadcasted_iota(jnp.int32, (tt, 1), 0)).astype(_F32)
        c, s_up, s_dn = _rope_tables(pos, inv_ref[...])
        tab_ref[0] = c
        tab_ref[1] = s_up
        tab_ref[2] = s_dn

    sh = mods_ref[0, 0:1, :]
    sc = mods_ref[0, 1:2, :]
    q, k, v, u, gates = _front_body(x_ref[0], sh, sc, n1g_ref[...], wqkv_ref, wglu_ref, wgate_ref,
                                    (tab_ref[0], tab_ref[1], tab_ref[2]))
    qkv_ref[0] = jnp.concatenate([q, k.astype(_BF16), v.astype(_BF16)], axis=1)
    u_ref[0] = u
    gates_ref[0] = gates
    kvlast_ref[0, 0] = jnp.concatenate([k[tt - WINDOW:], v[tt - WINDOW:]], axis=1)


def _front_prompt(x, mods, n1g, inv_lane, wqkv, wglu, wgate, tt):
    B, T, _ = x.shape
    grid = (T // tt, B)
    tok = lambda w: pl.BlockSpec((1, tt, w), lambda t, b: (b, t, 0))
    return pl.pallas_call(
        _front_prompt_kernel,
        grid=grid,
        in_specs=[
            tok(D_MODEL),
            pl.BlockSpec((1, 6, D_MODEL), lambda t, b: (b, 0, 0)),
            _const_spec((1, D_MODEL)),
            _const_spec((1, LANES)),
            _const_spec(wqkv.shape),
            _const_spec(wglu.shape),
            _const_spec(wgate.shape),
        ],
        out_specs=[
            tok(QKV_COLS),
            tok(CONV_DIM),
            tok(2 * D_MODEL),
            pl.BlockSpec((1, 1, WINDOW, 2 * KV_DIM), lambda t, b: (t, b, 0, 0)),
        ],
        out_shape=[
            jax.ShapeDtypeStruct((B, T, QKV_COLS), _BF16),
            jax.ShapeDtypeStruct((B, T, CONV_DIM), _F32),
            jax.ShapeDtypeStruct((B, T, 2 * D_MODEL), _BF16),
            jax.ShapeDtypeStruct((T // tt, B, WINDOW, 2 * KV_DIM), _F32),
        ],
        scratch_shapes=[pltpu.VMEM((3, tt, LANES), _F32)],
        compiler_params=pltpu.CompilerParams(
            dimension_semantics=("arbitrary", "arbitrary"), vmem_limit_bytes=VMEM_LIMIT),
        name="front_prompt",
    )(x, mods, n1g, inv_lane, wqkv, wglu, wgate)


def _order_after(x, anchor):
    bits = pltpu.bitcast(anchor[0:SUBLANES, 0:x.shape[1]], jnp.uint32)
    zero = pltpu.bitcast(lax.shift_right_logical(bits, jnp.uint32(32)), _F32)
    return jnp.concatenate([x[0:SUBLANES] + zero, x[SUBLANES:]], axis=0)


def _conv_post(dw, lng, lnb, anchor=None):
    mu = jnp.mean(dw, axis=-1, keepdims=True)
    d = dw - mu
    var = jnp.mean(d * d, axis=-1, keepdims=True)
    n = d * lax.rsqrt(var + LN_EPS) * lng + lnb
    s = n * _sigmoid(n)
    if anchor is not None:
        s = _order_after(s, anchor)
    return s.astype(_BF16)


def _ffn_chunks():
    step = 4 * 256
    return [(s, min(step, FFN_DIM - s)) for s in range(0, FFN_DIM, step)]


def _merge_body(x, a, s, gates, g1, sh2, sc2, n2g, wao_ref, wco_ref, wout_ref):
    a_out = _dot(a, wao_ref[...])
    cv_out = _dot(s, wco_ref[...])
    m = (gates[:, :D_MODEL].astype(_F32) * a_out + gates[:, D_MODEL:].astype(_F32) * cv_out).astype(_BF16)
    x1 = x + g1 * _dot(m, wout_ref[...])
    h2 = (_rms(x1, n2g) * (1.0 + sc2) + sh2).astype(_BF16)
    return x1, h2


def _ffn_body(x1, h2, g2, fng, wfi_ref, wfo_ref):
    acc = None
    for (c0, cw) in _ffn_chunks():
        g = _dot(h2, wfi_ref[:, c0:c0 + cw])
        uu = _dot(h2, wfi_ref[:, FFN_DIM + c0:FFN_DIM + c0 + cw])
        act = (g * _sigmoid(g) * uu).astype(_BF16)
        part = _dot(act, wfo_ref[c0:c0 + cw, :])
        acc = part if acc is None else acc + part
    x2 = x1 + g2 * acc
    return _rms(x2, fng)


def _softmax_sink(s, mask, sink):
    s = jnp.where(mask, s, NEG_INF)
    m = jnp.maximum(jnp.max(s, axis=-1, keepdims=True), sink)
    p = jnp.exp(s - m)
    den = jnp.sum(p, axis=-1, keepdims=True) + jnp.exp(sink - m)
    return p.astype(_BF16), den


def _softmax_sink_slot(s, mask, fill):
    s = jnp.where(mask, s, fill)
    p = jnp.exp(s - jnp.max(s, axis=-1, keepdims=True))
    return p.astype(_BF16), jnp.sum(p, axis=-1, keepdims=True)


def _softmax_sink_pv(s, mask, sink, vg):
    p, den = _softmax_sink(s, mask, sink)
    return _dot(p, vg) / den


CONV_CHUNK = 4 * SUBLANES


def _causal_conv(hist_ref, cw_ref, cb_ref, row0, row1):
    off = HIST_PAD - CONV_HIST
    chunk = CONV_CHUNK
    outs = []
    for c0 in range(row0, row1, chunk):
        acc_out = jnp.broadcast_to(cb_ref[...], (chunk, CONV_DIM))
        for r in range(SUBLANES):
            rows = chunk + (SUBLANES if r else 0)
            acc = None
            for a in range((off + CONV_WIDTH - 1 - r) // SUBLANES + 1):
                k = SUBLANES * a + r - off
                if k < 0:
                    continue
                term = hist_ref[c0 + SUBLANES * a:c0 + SUBLANES * a + rows, :] * cw_ref[k:k + 1, :]
                acc = term if acc is None else acc + term
            acc_out = acc_out + acc[r:r + chunk]
        outs.append(acc_out)
    return outs


def _back_prompt_kernel(nt, sinks_ref, x_ref, qkv_ref, u_ref, gates_ref, mods_ref,
                        cw_ref, cb_ref, lng_ref, lnb_ref, n2g_ref, fng_ref,
                        wao_ref, wco_ref, wout_ref, wfi_ref, wfo_ref,
                        y_ref, kvc_ref, hist_ref, x1_ref, h2_ref):
    i = pl.program_id(0)
    tile_a = jnp.minimum(i, pl.num_programs(0) - 2)
    t = tile_a % nt
    b_a = tile_a // nt
    b_b = jnp.maximum(i - 1, 0) // nt
    tt = x_ref.shape[1]
    nblk = tt // WINDOW

    @pl.when(i == 0)
    def _():
        x1_ref[...] = jnp.zeros_like(x1_ref)
        h2_ref[...] = jnp.zeros_like(h2_ref)

    @pl.when(t == 0)
    def _():
        kvc_ref[...] = jnp.zeros_like(kvc_ref)
        hist_ref[0:HIST_PAD, :] = jnp.zeros((HIST_PAD, CONV_DIM), _F32)

    md_a = lambda k: mods_ref[b_a, k:k + 1, :]
    md_b = lambda k: mods_ref[b_b, k:k + 1, :]
    x1_b = x1_ref[...]
    h2_b = h2_ref[...]
    chunks = _ffn_chunks()
    ffn_up = lambda c: (_dot(h2_b, wfi_ref[:, c[0]:c[0] + c[1]]),
                        _dot(h2_b, wfi_ref[:, FFN_DIM + c[0]:FFN_DIM + c[0] + c[1]]))
    ffn_down = lambda c, gu: _dot((gu[0] * _sigmoid(gu[0]) * gu[1]).astype(_BF16), wfo_ref[c[0]:c[0] + c[1], :])

    rows = lax.broadcasted_iota(jnp.int32, (GROUP * WINDOW, 2 * WINDOW), 0) % WINDOW
    cols = lax.broadcasted_iota(jnp.int32, (GROUP * WINDOW, 2 * WINDOW), 1)
    band = (cols > rows) & (cols <= rows + WINDOW)
    first_lo = WINDOW * (1 - jnp.minimum(t, 1))
    band_first = band & (cols >= first_lo)
    slot_col = lax.broadcasted_iota(jnp.int32, (WINDOW, 2 * WINDOW), 1) == 0
    fills = []
    for g in range(N_KV_HEADS):
        fills.append(jnp.concatenate(
            [jnp.where(slot_col, sinks_ref[g * GROUP + h], NEG_INF) for h in range(GROUP)], axis=0))
    slot_row = lax.broadcasted_iota(jnp.int32, (2 * WINDOW, HEAD_DIM), 0) == 0
    kv_prev = kvc_ref[...].astype(_BF16)
    units = []
    for j in range(nblk):
        r0 = j * WINDOW
        qb = qkv_ref[0, r0:r0 + WINDOW, 0:ATTN_DIM]
        kv_cur = qkv_ref[0, r0:r0 + WINDOW, ATTN_DIM:QKV_COLS]
        kv = jnp.concatenate([kv_prev, kv_cur], axis=0)
        for g in range(N_KV_HEADS):
            qg = jnp.concatenate(
                [qb[:, (g * GROUP + h) * HEAD_DIM:(g * GROUP + h + 1) * HEAD_DIM] for h in range(GROUP)],
                axis=0)
            kg = kv[:, g * HEAD_DIM:(g + 1) * HEAD_DIM]
            vg = kv[:, KV_DIM + g * HEAD_DIM:KV_DIM + (g + 1) * HEAD_DIM]
            vg = jnp.where(slot_row, jnp.zeros_like(vg), vg)
            units.append((_dot_nt(qg, kg), band_first if j == 0 else band, fills[g], vg))
        kv_prev = kv_cur
    kvc_ref[...] = kv_prev.astype(_F32)

    hist_ref[HIST_PAD:HIST_PAD + tt, :] = u_ref[0]
    half = tt // 2
    conv_branch = lambda r0, r1, anchor: _dot(
        _conv_post(jnp.concatenate(_causal_conv(hist_ref, cw_ref, cb_ref, r0, r1), axis=0),
                   lng_ref[...], lnb_ref[...], anchor), wco_ref[...])

    gu0 = ffn_up(chunks[0])
    outs = []
    for (s, mask, fill, vg) in units:
        p, den = _softmax_sink_slot(s, mask, fill)
        outs.append(_dot(p, vg) / den)
    gu1 = ffn_up(chunks[1])
    cv_lo = conv_branch(0, half, gu1[1])
    down0 = ffn_down(chunks[0], gu0)
    acc = down0

    a_blocks = []
    for j in range(nblk):
        heads = []
        for g in range(N_KV_HEADS):
            o = outs[j * N_KV_HEADS + g]
            heads += [o[h * WINDOW:(h + 1) * WINDOW] for h in range(GROUP)]
        a_blocks.append(jnp.concatenate(heads, axis=1))
    a = jnp.concatenate(a_blocks, axis=0).astype(_BF16)

    a_out = _dot(a, wao_ref[...])
    cv_out = jnp.concatenate([cv_lo, conv_branch(half, tt, down0)], axis=0)
    hist_ref[0:HIST_PAD, :] = hist_ref[tt:tt + HIST_PAD, :]
    acc = acc + ffn_down(chunks[1], gu1)
    gu2 = ffn_up(chunks[2])
    gates = gates_ref[0]
    m = (gates[:, :D_MODEL].astype(_F32) * a_out + gates[:, D_MODEL:].astype(_F32) * cv_out).astype(_BF16)
    x1 = x_ref[0] + md_a(2) * _dot(m, wout_ref[...])
    acc = acc + ffn_down(chunks[2], gu2)
    h2 = (_rms(x1, n2g_ref[...]) * (1.0 + md_a(4)) + md_a(3)).astype(_BF16)

    y_ref[0] = _rms(x1_b + md_b(5) * acc, fng_ref[...])
    x1_ref[...] = x1
    h2_ref[...] = h2


def _back_prompt(sinks, x, qkv, u, gates, mods, cw, cb, lng, lnb, n2g, fng, wao, wco, wout, wfi, wfo, tt):
    B, T, _ = x.shape
    nt = T // tt
    n_tiles = B * nt
    tile_in = lambda i: jnp.minimum(i, n_tiles - 1)
    tile_out = lambda i: jnp.maximum(i - 1, 0)
    tok_in = lambda w: pl.BlockSpec((1, tt, w), lambda i: (tile_in(i) // nt, tile_in(i) % nt, 0))
    consts = [cw, cb, lng, lnb, n2g, fng, wao, wco, wout, wfi, wfo]
    return pl.pallas_call(
        functools.partial(_back_prompt_kernel, nt),
        grid=(n_tiles + 1,),
        in_specs=[
            pl.BlockSpec(memory_space=pltpu.SMEM),
            tok_in(D_MODEL), tok_in(QKV_COLS), tok_in(CONV_DIM), tok_in(2 * D_MODEL),
            _const_spec(mods.shape),
        ] + [_const_spec(c.shape) for c in consts],
        out_specs=pl.BlockSpec((1, tt, D_MODEL), lambda i: (tile_out(i) // nt, tile_out(i) % nt, 0)),
        out_shape=jax.ShapeDtypeStruct((B, T, D_MODEL), _F32),
        scratch_shapes=[
            pltpu.VMEM((WINDOW, 2 * KV_DIM), _F32),
            pltpu.VMEM((HIST_PAD + tt, CONV_DIM), _F32),
            pltpu.VMEM((tt, D_MODEL), _F32),
            pltpu.VMEM((tt, D_MODEL), _BF16),
        ],
        compiler_params=pltpu.CompilerParams(
            dimension_semantics=("arbitrary",), vmem_limit_bytes=VMEM_LIMIT),
        name="back_prompt",
    )(sinks, x, qkv, u, gates, mods, *consts)


def _front_sample_kernel(x_ref, mods_ref, n1g_ref, inv_ref, wqkv_ref, wglu_ref, wgate_ref,
                         q_ref, kv_ref, u_ref, gates_ref):
    S = x_ref.shape[0]
    nb = x_ref.shape[1]
    pos = (PAST_LEN + lax.broadcasted_iota(jnp.int32, (8, 1), 0)).astype(_F32)
    c8, up8, dn8 = _rope_tables(pos, inv_ref[...])
    expand = lambda tab: jnp.concatenate(
        [jnp.broadcast_to(tab[s:s + 1, :], (nb, LANES)) for s in range(S)], axis=0)
    tabs = (expand(c8), expand(up8), expand(dn8))
    x = x_ref[...].reshape(S * nb, D_MODEL)
    sh = jnp.concatenate([mods_ref[0]] * S, axis=0)
    sc = jnp.concatenate([mods_ref[1]] * S, axis=0)
    q, k, v, u, gates = _front_body(x, sh, sc, n1g_ref[...], wqkv_ref, wglu_ref, wgate_ref, tabs)
    q_ref[...] = q.reshape(S, nb, ATTN_DIM)
    kv_ref[...] = jnp.concatenate([k, v], axis=1).reshape(S, nb, 2 * KV_DIM)
    u_ref[...] = u.reshape(S, nb, CONV_DIM)
    gates_ref[...] = gates.reshape(S, nb, 2 * D_MODEL)


def _front_sample(x, mods, n1g, inv_lane, wqkv, wglu, wgate):
    S, nb, _ = x.shape
    full = lambda shape: pl.BlockSpec(shape, lambda i: (0,) * len(shape))
    args = (x, mods, n1g, inv_lane, wqkv, wglu, wgate)
    return pl.pallas_call(
        _front_sample_kernel,
        grid=(1,),
        in_specs=[full(a.shape) for a in args],
        out_specs=[full((S, nb, ATTN_DIM)), full((S, nb, 2 * KV_DIM)), full((S, nb, CONV_DIM)),
                   full((S, nb, 2 * D_MODEL))],
        out_shape=[
            jax.ShapeDtypeStruct((S, nb, ATTN_DIM), _BF16),
            jax.ShapeDtypeStruct((S, nb, 2 * KV_DIM), _F32),
            jax.ShapeDtypeStruct((S, nb, CONV_DIM), _F32),
            jax.ShapeDtypeStruct((S, nb, 2 * D_MODEL), _BF16),
        ],
        compiler_params=pltpu.CompilerParams(
            dimension_semantics=("arbitrary",), vmem_limit_bytes=VMEM_LIMIT),
        name="front_sample",
    )(*args)


def _attn_sample_kernel(sinks_ref, q_ref, kvn_ref, kbuf_ref, vbuf_ref, a_ref, kout_ref, vout_ref):
    S = q_ref.shape[0]
    bc = q_ref.shape[1]
    L = kbuf_ref.shape[1]
    npad = 8
    rows = lax.broadcasted_iota(jnp.int32, (GROUP * S, L + npad), 0) % S
    cols = lax.broadcasted_iota(jnp.int32, (GROUP * S, L + npad), 1)
    mask = (cols > rows) & (cols <= rows + WINDOW) & (cols < L + S)
    sink_cols = []
    for g in range(N_KV_HEADS):
        sink_cols.append(jnp.concatenate(
            [jnp.full((S, 1), sinks_ref[g * GROUP + h], _F32) for h in range(GROUP)], axis=0))
    zpad = jnp.zeros((npad - S, 2 * KV_DIM), _F32)
    for b in range(bc):
        kvn = jnp.concatenate([kvn_ref[:, b, :], zpad], axis=0)
        kold = kbuf_ref[b]
        vold = vbuf_ref[b]
        kc = jnp.concatenate([kold, kvn[:, :KV_DIM]], axis=0)
        vc = jnp.concatenate([vold, kvn[:, KV_DIM:]], axis=0)
        kout_ref[b] = kc[S:S + L]
        vout_ref[b] = vc[S:S + L]
        kcb = kc.astype(_BF16)
        vcb = vc.astype(_BF16)
        qb = q_ref[:, b, :]
        heads = []
        for g in range(N_KV_HEADS):
            qg = jnp.concatenate(
                [qb[:, (g * GROUP + h) * HEAD_DIM:(g * GROUP + h + 1) * HEAD_DIM] for h in range(GROUP)],
                axis=0)
            kg = kcb[:, g * HEAD_DIM:(g + 1) * HEAD_DIM]
            vg = vcb[:, g * HEAD_DIM:(g + 1) * HEAD_DIM]
            o = _softmax_sink_pv(_dot_nt(qg, kg), mask, sink_cols[g], vg)
            heads += [o[h * S:(h + 1) * S] for h in range(GROUP)]
        a_ref[:, b, :] = jnp.concatenate(heads, axis=1).astype(_BF16)


def _attn_sample(sinks, q, kvn, kbuf, vbuf, bc):
    S, nb, _ = q.shape
    L = kbuf.shape[1]
    stepb = lambda w: pl.BlockSpec((S, bc, w), lambda i: (0, i, 0))
    buf = pl.BlockSpec((bc, L, KV_DIM), lambda i: (i, 0, 0))
    return pl.pallas_call(
        _attn_sample_kernel,
        grid=(nb // bc,),
        in_specs=[pl.BlockSpec(memory_space=pltpu.SMEM), stepb(ATTN_DIM), stepb(2 * KV_DIM), buf, buf],
        out_specs=[stepb(ATTN_DIM), buf, buf],
        out_shape=[
            jax.ShapeDtypeStruct((S, nb, ATTN_DIM), _BF16),
            jax.ShapeDtypeStruct((nb, L, KV_DIM), _F32),
            jax.ShapeDtypeStruct((nb, L, KV_DIM), _F32),
        ],
        compiler_params=pltpu.CompilerParams(dimension_semantics=("arbitrary",)),
        name="attn_sample",
    )(sinks, q, kvn, kbuf, vbuf)


def _back_sample_kernel(x_ref, a_ref, u_ref, gates_ref, mods_ref, hist_ref,
                        cw_ref, cb_ref, lng_ref, lnb_ref, n2g_ref, fng_ref,
                        wao_ref, wco_ref, wout_ref, wfi_ref, wfo_ref,
                        y_ref, hout_ref):
    S = x_ref.shape[0]
    nb = x_ref.shape[1]
    slab = lambda j: hist_ref[j] if j < CONV_HIST else u_ref[j - CONV_HIST]
    outs = []
    for s in range(S):
        dw = jnp.broadcast_to(cb_ref[...], (nb, CONV_DIM))
        for k in range(CONV_WIDTH):
            dw = dw + slab(s + k) * cw_ref[k:k + 1, :]
        outs.append(dw)
    for j in range(CONV_HIST):
        hout_ref[j] = slab(j + S)
    s_act = _conv_post(jnp.concatenate(outs, axis=0), lng_ref[...], lnb_ref[...])

    md = lambda i: jnp.concatenate([mods_ref[i]] * S, axis=0)
    x1, h2 = _merge_body(x_ref[...].reshape(S * nb, D_MODEL), a_ref[...].reshape(S * nb, ATTN_DIM), s_act,
                         gates_ref[...].reshape(S * nb, 2 * D_MODEL), md(2), md(3), md(4),
                         n2g_ref[...], wao_ref, wco_ref, wout_ref)
    y = _ffn_body(x1, h2, md(5), fng_ref[...], wfi_ref, wfo_ref)
    y_ref[...] = y.reshape(S, nb, D_MODEL)


def _back_sample(x, a, u, gates, mods, hist, cw, cb, lng, lnb, n2g, fng, wao, wco, wout, wfi, wfo):
    S, nb, _ = x.shape
    full = lambda shape: pl.BlockSpec(shape, lambda i: (0,) * len(shape))
    args = (x, a, u, gates, mods, hist, cw, cb, lng, lnb, n2g, fng, wao, wco, wout, wfi, wfo)
    return pl.pallas_call(
        _back_sample_kernel,
        grid=(1,),
        in_specs=[full(v.shape) for v in args],
        out_specs=[full((S, nb, D_MODEL)), full((CONV_HIST, nb, CONV_DIM))],
        out_shape=[
            jax.ShapeDtypeStruct((S, nb, D_MODEL), _F32),
            jax.ShapeDtypeStruct((CONV_HIST, nb, CONV_DIM), _F32),
        ],
        compiler_params=pltpu.CompilerParams(
            dimension_semantics=("arbitrary",), vmem_limit_bytes=VMEM_LIMIT),
        name="back_sample",
    )(*args)


def _inv_lane():
    inv = ROPE_THETA ** (-jnp.arange(0, ROPE_DIM, 2, dtype=_F32) / ROPE_DIM)
    return jnp.tile(inv, LANES // ROPE_HALF).reshape(1, LANES)


def _layer(xp, xs_t, kbuf, vbuf, cbuf_t, mods_p, mods_s_t, lw, final_g, tt_front, tt_back, bc):
    row = lambda v: v.reshape(1, -1)
    w_in = lw['w_in'].astype(_BF16)
    wqkv = w_in[:, :QKV_COLS]
    wglu = w_in[:, QKV_COLS:QKV_COLS + 2 * CONV_DIM]
    wgate = w_in[:, QKV_COLS + 2 * CONV_DIM:]
    wao = lw['w_attn_o'].astype(_BF16)
    wco = lw['w_conv_o'].astype(_BF16)
    wout = lw['w_out'].astype(_BF16)
    wfi = lw['w_ffn_in'].astype(_BF16)
    wfo = lw['w_ffn_out'].astype(_BF16)
    inv_lane = _inv_lane()
    n1g, n2g, fng = row(lw['norm1_g']), row(lw['norm2_g']), row(final_g)
    cw, cb, lng, lnb = lw['conv_w'], row(lw['conv_b']), row(lw['conv_ln_g']), row(lw['conv_ln_b'])
    sinks = lw['sinks']

    qkv, u, gates, kvlast = _front_prompt(xp, mods_p, n1g, inv_lane, wqkv, wglu, wgate, tt_front)
    yp = _back_prompt(sinks, xp, qkv, u, gates, mods_p, cw, cb, lng, lnb, n2g, fng,
                      wao, wco, wout, wfi, wfo, tt_back)

    q_s, kvn_s, u_s, gates_s = _front_sample(xs_t, mods_s_t, n1g, inv_lane, wqkv, wglu, wgate)
    a_s, kout, vout = _attn_sample(sinks, q_s, kvn_s, kbuf, vbuf, bc)
    ys_t, hout_t = _back_sample(xs_t, a_s, u_s, gates_s, mods_s_t, cbuf_t, cw, cb, lng, lnb, n2g, fng,
                                wao, wco, wout, wfi, wfo)
    return yp, ys_t, kvlast, u, kout, vout, hout_t


def kernel(x_prompt, x_sample, state_k_win, state_v_win, state_conv, c_prompt, c_sample, norm1_g, norm2_g,
           w_ada, b_ada, w_in, sinks, w_attn_o, conv_w, conv_b, conv_ln_g, conv_ln_b, w_conv_o, w_out,
           w_ffn_in, w_ffn_out, final_norm_g):
    B, T, _ = x_prompt.shape
    nb, S, _ = x_sample.shape
    depth = w_in.shape[0]
    assert depth == 1, "final norm is fused into the single trunk layer"
    L = state_k_win.shape[2]
    l = 0
    lw = dict(norm1_g=norm1_g[l], norm2_g=norm2_g[l], w_in=w_in[l], sinks=sinks[l], w_attn_o=w_attn_o[l],
              conv_w=conv_w[l], conv_b=conv_b[l], conv_ln_g=conv_ln_g[l], conv_ln_b=conv_ln_b[l],
              w_conv_o=w_conv_o[l], w_out=w_out[l], w_ffn_in=w_ffn_in[l], w_ffn_out=w_ffn_out[l])

    mods = _adaln_mods(jnp.concatenate([c_prompt, c_sample], axis=0), w_ada[l], b_ada[l])
    mods = mods.reshape(B + nb, 6, D_MODEL)
    mods_p = mods[:B]
    mods_s_t = mods[B:].transpose(1, 0, 2)

    xs_t = x_sample.transpose(1, 0, 2)
    kbuf = state_k_win[l].reshape(nb, L, KV_DIM)
    vbuf = state_v_win[l].reshape(nb, L, KV_DIM)
    cbuf_t = state_conv[l].transpose(1, 0, 2)

    tt_front = min(512, T)
    tt_back = min(256, T)
    yp, ys_t, kvtiles, u, kout, vout, hout_t = _layer(
        x_prompt, xs_t, kbuf, vbuf, cbuf_t, mods_p, mods_s_t, lw, final_norm_g, tt_front, tt_back, bc=8)

    y_sample = ys_t.transpose(1, 0, 2)
    kvlast = kvtiles[-1]
    k_win_prompt = kvlast[:, :, :KV_DIM].reshape(1, B, WINDOW, N_KV_HEADS, HEAD_DIM)
    v_win_prompt = kvlast[:, :, KV_DIM:].reshape(1, B, WINDOW, N_KV_HEADS, HEAD_DIM)
    conv_prompt = u[:, T - CONV_HIST:, :][None]
    k_win_sample = kout.reshape(1, nb, L, N_KV_HEADS, HEAD_DIM)
    v_win_sample = vout.reshape(1, nb, L, N_KV_HEADS, HEAD_DIM)
    conv_sample = hout_t.transpose(1, 0, 2)[None]
    return (yp, y_sample, k_win_prompt, v_win_prompt, conv_prompt, k_win_sample, v_win_sample, conv_sample)
```

```python
import functools

import jax
import jax.numpy as jnp
import numpy as np
from jax import lax
from jax.experimental import pallas as pl
from jax.experimental.pallas import tpu as pltpu

D_MODEL = 1024
HEAD_DIM = 64
N_HEADS = 8
N_KV_HEADS = 2
GROUP = N_HEADS // N_KV_HEADS
ATTN_DIM = N_HEADS * HEAD_DIM
KV_DIM = N_KV_HEADS * HEAD_DIM
WINDOW = 128
ROPE_THETA = 500000.0
ROPE_DIM = HEAD_DIM // 4
ROPE_HALF = ROPE_DIM // 2
CONV_DIM = D_MODEL // 2
CONV_WIDTH = 31
CONV_HIST = CONV_WIDTH - 1
FFN_DIM = -(-8 * D_MODEL // (3 * 256)) * 256
PAST_LEN = 16384
RMS_EPS = 1e-6
LN_EPS = 1e-5
NEG_INF = -1e30

LANES = 128
SUBLANES = 8
HIST_PAD = 32
QK_COLS = ATTN_DIM + KV_DIM
QKV_COLS = ATTN_DIM + 2 * KV_DIM
VMEM_LIMIT = 56 * 1024 * 1024

_F32 = jnp.float32
_BF16 = jnp.bfloat16


def _dot(a, b):
    return jnp.dot(a, b, preferred_element_type=_F32)


def _dot_nt(a, b):
    return lax.dot_general(a, b, (((1,), (1,)), ((), ())), preferred_element_type=_F32)


def _rms(x, g):
    return x * lax.rsqrt(jnp.mean(x * x, axis=-1, keepdims=True) + RMS_EPS) * g


def _sigmoid(x):
    return 1.0 / (1.0 + jnp.exp(-x))


def _const_spec(shape):
    nd = len(shape)
    return pl.BlockSpec(shape, lambda *_: (0,) * nd, pipeline_mode=pl.Buffered(1))


def _mods_kernel(c_ref, w_ref, b_ref, o_ref):
    c = c_ref[...]
    a = (c * _sigmoid(c)).astype(_BF16)
    o_ref[...] = _dot(a, w_ref[...].astype(_BF16)) + b_ref[...]


def _adaln_mods(c, w_ada, b_ada):
    n = c.shape[0]
    cols = w_ada.shape[1]
    bn = D_MODEL
    return pl.pallas_call(
        _mods_kernel,
        grid=(cols // bn,),
        in_specs=[
            pl.BlockSpec((n, D_MODEL), lambda j: (0, 0)),
            pl.BlockSpec((D_MODEL, bn), lambda j: (0, j)),
            pl.BlockSpec((1, bn), lambda j: (0, j)),
        ],
        out_specs=pl.BlockSpec((n, bn), lambda j: (0, j)),
        out_shape=jax.ShapeDtypeStruct((n, cols), _F32),
        compiler_params=pltpu.CompilerParams(dimension_semantics=("arbitrary",)),
        name="adaln_mods",
    )(c, w_ada, b_ada.reshape(1, cols))


def _rope_tables(pos, inv_lane):
    ang = pos * inv_lane
    cos = jnp.cos(ang)
    sin = jnp.sin(ang)
    lane = lax.broadcasted_iota(jnp.int32, (1, LANES), 1) % HEAD_DIM
    c = jnp.where(lane < ROPE_DIM, cos, 1.0)
    s_up = jnp.where((lane >= ROPE_HALF) & (lane < ROPE_DIM), sin, 0.0)
    s_dn = jnp.where(lane < ROPE_HALF, -sin, 0.0)
    return c, s_up, s_dn


def _apply_rope(z, c, s_up, s_dn):
    outs = []
    for j in range(z.shape[1] // LANES):
        zz = z[:, j * LANES:(j + 1) * LANES]
        outs.append(zz * c + pltpu.roll(zz, ROPE_HALF, 1) * s_up
                    + pltpu.roll(zz, LANES - ROPE_HALF, 1) * s_dn)
    return jnp.concatenate(outs, axis=1)


def _front_body(x, sh, sc, n1g, wqkv_ref, wglu_ref, wgate_ref, tabs):
    h = (_rms(x, n1g) * (1.0 + sc) + sh).astype(_BF16)
    zqkv = _dot(h, wqkv_ref[...])
    qk = _apply_rope(zqkv[:, :QK_COLS], *tabs)
    q = (qk[:, :ATTN_DIM] * (HEAD_DIM ** -0.5)).astype(_BF16)
    k = qk[:, ATTN_DIM:]
    v = zqkv[:, QK_COLS:]
    zg = _dot(h, wglu_ref[...])
    u = zg[:, :CONV_DIM] * _sigmoid(zg[:, CONV_DIM:])
    gates = _sigmoid(_dot(h, wgate_ref[...])).astype(_BF16)
    return q, k, v, u, gates


def _front_prompt_kernel(x_ref, mods_ref, n1g_ref, inv_ref, wqkv_ref, wglu_ref, wgate_ref,
                         qkv_ref, u_ref, gates_ref, kvlast_ref, tab_ref):
    t = pl.program_id(0)
    b = pl.program_id(1)
    tt = x_ref.shape[1]

    @pl.when(b == 0)
    def _():
        pos = (t * tt + lax.broadcasted_iota(jnp.int32, (tt, 1), 0)).astype(_F32)
        c, s_up, s_dn = _rope_tables(pos, inv_ref[...])
        tab_ref[0] = c
        tab_ref[1] = s_up
        tab_ref[2] = s_dn

    sh = mods_ref[0, 0:1, :]
    sc = mods_ref[0, 1:2, :]
    half = tt // 2
    for r0 in (0, half):
        rows = slice(r0, r0 + half)
        q, k, v, u, gates = _front_body(x_ref[0, rows, :], sh, sc, n1g_ref[...], wqkv_ref, wglu_ref, wgate_ref,
                                        (tab_ref[0, rows, :], tab_ref[1, rows, :], tab_ref[2, rows, :]))
        qkv_ref[0, rows, :] = jnp.concatenate([q, k.astype(_BF16), v.astype(_BF16)], axis=1)
        u_ref[0, rows, :] = u
        gates_ref[0, rows, :] = gates
    kvlast_ref[0, 0] = jnp.concatenate([k[half - WINDOW:], v[half - WINDOW:]], axis=1)


def _front_prompt(x, mods, n1g, inv_lane, wqkv, wglu, wgate, tt):
    B, T, _ = x.shape
    grid = (T // tt, B)
    tok = lambda w: pl.BlockSpec((1, tt, w), lambda t, b: (b, t, 0))
    return pl.pallas_call(
        _front_prompt_kernel,
        grid=grid,
        in_specs=[
            tok(D_MODEL),
            pl.BlockSpec((1, 6, D_MODEL), lambda t, b: (b, 0, 0)),
            _const_spec((1, D_MODEL)),
            _const_spec((1, LANES)),
            _const_spec(wqkv.shape),
            _const_spec(wglu.shape),
            _const_spec(wgate.shape),
        ],
        out_specs=[
            tok(QKV_COLS),
            tok(CONV_DIM),
            tok(2 * D_MODEL),
            pl.BlockSpec((1, 1, WINDOW, 2 * KV_DIM), lambda t, b: (t, b, 0, 0)),
        ],
        out_shape=[
            jax.ShapeDtypeStruct((B, T, QKV_COLS), _BF16),
            jax.ShapeDtypeStruct((B, T, CONV_DIM), _F32),
            jax.ShapeDtypeStruct((B, T, 2 * D_MODEL), _BF16),
            jax.ShapeDtypeStruct((T // tt, B, WINDOW, 2 * KV_DIM), _F32),
        ],
        scratch_shapes=[pltpu.VMEM((3, tt, LANES), _F32)],
        compiler_params=pltpu.CompilerParams(
            dimension_semantics=("arbitrary", "arbitrary"), vmem_limit_bytes=VMEM_LIMIT),
        name="front_prompt",
    )(x, mods, n1g, inv_lane, wqkv, wglu, wgate)


def _order_after(x, anchor):
    bits = pltpu.bitcast(anchor[0:SUBLANES, 0:x.shape[1]], jnp.uint32)
    zero = pltpu.bitcast(lax.shift_right_logical(bits, jnp.uint32(32)), _F32)
    return jnp.concatenate([x[0:SUBLANES] + zero, x[SUBLANES:]], axis=0)


def _conv_post(dw, lng, lnb, anchor=None):
    mu = jnp.mean(dw, axis=-1, keepdims=True)
    d = dw - mu
    var = jnp.mean(d * d, axis=-1, keepdims=True)
    n = d * lax.rsqrt(var + LN_EPS) * lng + lnb
    s = n * _sigmoid(n)
    if anchor is not None:
        s = _order_after(s, anchor)
    return s.astype(_BF16)


def _ffn_chunks():
    step = 4 * 256
    return [(s, min(step, FFN_DIM - s)) for s in range(0, FFN_DIM, step)]


def _merge_body(x, a, s, gates, g1, sh2, sc2, n2g, wao_ref, wco_ref, wout_ref):
    a_out = _dot(a, wao_ref[...])
    cv_out = _dot(s, wco_ref[...])
    m = (gates[:, :D_MODEL].astype(_F32) * a_out + gates[:, D_MODEL:].astype(_F32) * cv_out).astype(_BF16)
    x1 = x + g1 * _dot(m, wout_ref[...])
    h2 = (_rms(x1, n2g) * (1.0 + sc2) + sh2).astype(_BF16)
    return x1, h2


def _ffn_body(x1, h2, g2, fng, wfi_ref, wfo_ref):
    acc = None
    for (c0, cw) in _ffn_chunks():
        g = _dot(h2, wfi_ref[:, c0:c0 + cw])
        uu = _dot(h2, wfi_ref[:, FFN_DIM + c0:FFN_DIM + c0 + cw])
        act = (g * _sigmoid(g) * uu).astype(_BF16)
        part = _dot(act, wfo_ref[c0:c0 + cw, :])
        acc = part if acc is None else acc + part
    x2 = x1 + g2 * acc
    return _rms(x2, fng)


def _softmax_sink_slot(s, mask, fill):
    s = jnp.where(mask, s, fill)
    p = jnp.exp(s - jnp.max(s, axis=-1, keepdims=True))
    return p.astype(_BF16), jnp.sum(p, axis=-1, keepdims=True)


CONV_CHUNK = 4 * SUBLANES


def _causal_conv(hist_ref, cw_ref, cb_ref, row0, row1):
    off = HIST_PAD - CONV_HIST
    chunk = CONV_CHUNK
    outs = []
    for c0 in range(row0, row1, chunk):
        acc_out = jnp.broadcast_to(cb_ref[...], (chunk, CONV_DIM))
        for r in range(SUBLANES):
            rows = chunk + (SUBLANES if r else 0)
            acc = None
            for a in range((off + CONV_WIDTH - 1 - r) // SUBLANES + 1):
                k = SUBLANES * a + r - off
                if k < 0:
                    continue
                r_lo = c0 + SUBLANES * a
                term = hist_ref[r_lo:r_lo + rows, :] * cw_ref[k:k + 1, :]
                acc = term if acc is None else acc + term
            acc_out = acc_out + acc[r:r + chunk]
        outs.append(acc_out)
    return outs


def _back_prompt_kernel(nt, sinks_ref, x_ref, qkv_ref, u_ref, gates_ref, mods_ref,
                        cw_ref, cb_ref, lng_ref, lnb_ref, n2g_ref, fng_ref,
                        wao_ref, wco_ref, wout_ref, wfi_ref, wfo_ref,
                        y_ref, kvc_ref, hist_ref, x1_ref, h2_ref):
    i = pl.program_id(0)
    tile_a = jnp.minimum(i, pl.num_programs(0) - 2)
    t = tile_a % nt
    b_a = tile_a // nt
    b_b = jnp.maximum(i - 1, 0) // nt
    tt = x_ref.shape[1]
    nblk = tt // WINDOW

    @pl.when(i == 0)
    def _():
        x1_ref[...] = jnp.zeros_like(x1_ref)
        h2_ref[...] = jnp.zeros_like(h2_ref)

    @pl.when(t == 0)
    def _():
        kvc_ref[...] = jnp.zeros_like(kvc_ref)
        hist_ref[0:HIST_PAD, :] = jnp.zeros((HIST_PAD, CONV_DIM), _F32)

    md_a = lambda k: mods_ref[b_a, k:k + 1, :]
    md_b = lambda k: mods_ref[b_b, k:k + 1, :]
    x1_b = x1_ref[...]
    h2_b = h2_ref[...]
    chunks = _ffn_chunks()
    ffn_up = lambda c: (_dot(h2_b, wfi_ref[:, c[0]:c[0] + c[1]]),
                        _dot(h2_b, wfi_ref[:, FFN_DIM + c[0]:FFN_DIM + c[0] + c[1]]))
    ffn_down = lambda c, gu: _dot((gu[0] * _sigmoid(gu[0]) * gu[1]).astype(_BF16), wfo_ref[c[0]:c[0] + c[1], :])

    rows = lax.broadcasted_iota(jnp.int32, (GROUP * WINDOW, 2 * WINDOW), 0) % WINDOW
    cols = lax.broadcasted_iota(jnp.int32, (GROUP * WINDOW, 2 * WINDOW), 1)
    band = (cols > rows) & (cols <= rows + WINDOW)
    first_lo = WINDOW * (1 - jnp.minimum(t, 1))
    band_first = band & (cols >= first_lo)
    slot_col = lax.broadcasted_iota(jnp.int32, (WINDOW, 2 * WINDOW), 1) == 0
    fills = []
    for g in range(N_KV_HEADS):
        fills.append(jnp.concatenate(
            [jnp.where(slot_col, sinks_ref[g * GROUP + h], NEG_INF) for h in range(GROUP)], axis=0))
    slot_row = lax.broadcasted_iota(jnp.int32, (2 * WINDOW, HEAD_DIM), 0) == 0
    kv_prev = kvc_ref[...].astype(_BF16)
    units = []
    for j in range(nblk):
        r0 = j * WINDOW
        qb = qkv_ref[0, r0:r0 + WINDOW, 0:ATTN_DIM]
        kv_cur = qkv_ref[0, r0:r0 + WINDOW, ATTN_DIM:QKV_COLS]
        kv = jnp.concatenate([kv_prev, kv_cur], axis=0)
        for g in range(N_KV_HEADS):
            qg = jnp.concatenate(
                [qb[:, (g * GROUP + h) * HEAD_DIM:(g * GROUP + h + 1) * HEAD_DIM] for h in range(GROUP)],
                axis=0)
            kg = kv[:, g * HEAD_DIM:(g + 1) * HEAD_DIM]
            vg = kv[:, KV_DIM + g * HEAD_DIM:KV_DIM + (g + 1) * HEAD_DIM]
            vg = jnp.where(slot_row, jnp.zeros_like(vg), vg)
            units.append((_dot_nt(qg, kg), band_first if j == 0 else band, fills[g], vg))
        kv_prev = kv_cur
    kvc_ref[...] = kv_prev.astype(_F32)

    hist_ref[HIST_PAD:HIST_PAD + tt, :] = u_ref[0]
    half = tt // 2
    conv_branch = lambda r0, r1, anchor: _dot(
        _conv_post(jnp.concatenate(_causal_conv(hist_ref, cw_ref, cb_ref, r0, r1), axis=0),
                   lng_ref[...], lnb_ref[...], anchor), wco_ref[...])

    gu0 = ffn_up(chunks[0])
    outs = []
    for (s, mask, fill, vg) in units:
        p, den = _softmax_sink_slot(s, mask, fill)
        outs.append(_dot(p, vg) / den)
    gu1 = ffn_up(chunks[1])
    cv_lo = conv_branch(0, half, gu1[1])
    down0 = ffn_down(chunks[0], gu0)
    acc = down0

    a_blocks = []
    for j in range(nblk):
        heads = []
        for g in range(N_KV_HEADS):
            o = outs[j * N_KV_HEADS + g]
            heads += [o[h * WINDOW:(h + 1) * WINDOW] for h in range(GROUP)]
        a_blocks.append(jnp.concatenate(heads, axis=1))
    a = jnp.concatenate(a_blocks, axis=0).astype(_BF16)

    a_out = _dot(a, wao_ref[...])
    cv_out = jnp.concatenate([cv_lo, conv_branch(half, tt, down0)], axis=0)
    hist_ref[0:HIST_PAD, :] = hist_ref[tt:tt + HIST_PAD, :]
    acc = acc + ffn_down(chunks[1], gu1)
    gu2 = ffn_up(chunks[2])
    gates = gates_ref[0]
    m = (gates[:, :D_MODEL].astype(_F32) * a_out + gates[:, D_MODEL:].astype(_F32) * cv_out).astype(_BF16)
    x1 = x_ref[0] + md_a(2) * _dot(m, wout_ref[...])
    acc = acc + ffn_down(chunks[2], gu2)
    h2 = (_rms(x1, n2g_ref[...]) * (1.0 + md_a(4)) + md_a(3)).astype(_BF16)

    y_ref[0] = _rms(x1_b + md_b(5) * acc, fng_ref[...])
    x1_ref[...] = x1
    h2_ref[...] = h2


def _back_prompt(sinks, x, qkv, u, gates, mods, cw, cb, lng, lnb, n2g, fng, wao, wco, wout, wfi, wfo, tt):
    B, T, _ = x.shape
    nt = T // tt
    n_tiles = B * nt
    tile_in = lambda i: jnp.minimum(i, n_tiles - 1)
    tile_out = lambda i: jnp.maximum(i - 1, 0)
    tok_in = lambda w: pl.BlockSpec((1, tt, w), lambda i: (tile_in(i) // nt, tile_in(i) % nt, 0))
    consts = [cw, cb, lng, lnb, n2g, fng, wao, wco, wout, wfi, wfo]
    return pl.pallas_call(
        functools.partial(_back_prompt_kernel, nt),
        grid=(n_tiles + 1,),
        in_specs=[
            pl.BlockSpec(memory_space=pltpu.SMEM),
            tok_in(D_MODEL), tok_in(QKV_COLS), tok_in(CONV_DIM), tok_in(2 * D_MODEL),
            _const_spec(mods.shape),
        ] + [_const_spec(c.shape) for c in consts],
        out_specs=pl.BlockSpec((1, tt, D_MODEL), lambda i: (tile_out(i) // nt, tile_out(i) % nt, 0)),
        out_shape=jax.ShapeDtypeStruct((B, T, D_MODEL), _F32),
        scratch_shapes=[
            pltpu.VMEM((WINDOW, 2 * KV_DIM), _F32),
            pltpu.VMEM((HIST_PAD + tt, CONV_DIM), _F32),
            pltpu.VMEM((tt, D_MODEL), _F32),
            pltpu.VMEM((tt, D_MODEL), _BF16),
        ],
        compiler_params=pltpu.CompilerParams(
            dimension_semantics=("arbitrary",), vmem_limit_bytes=VMEM_LIMIT),
        name="back_prompt",
    )(sinks, x, qkv, u, gates, mods, *consts)


def _front_sample_kernel(x_ref, mods_ref, n1g_ref, inv_ref, wqkv_ref, wglu_ref, wgate_ref,
                         q_ref, kv_ref, u_ref, gates_ref):
    nb = x_ref.shape[0]
    S = x_ref.shape[1]
    pos = (PAST_LEN + lax.broadcasted_iota(jnp.int32, (8, 1), 0)).astype(_F32)
    c8, up8, dn8 = _rope_tables(pos, inv_ref[...])
    expand = lambda tab: jnp.concatenate(
        [jnp.broadcast_to(tab[s:s + 1, :], (nb, LANES)) for s in range(S)], axis=0)
    tabs = (expand(c8), expand(up8), expand(dn8))
    x = jnp.concatenate([x_ref[:, s, :] for s in range(S)], axis=0)
    sh = jnp.concatenate([mods_ref[:, 0, :]] * S, axis=0)
    sc = jnp.concatenate([mods_ref[:, 1, :]] * S, axis=0)
    q, k, v, u, gates = _front_body(x, sh, sc, n1g_ref[...], wqkv_ref, wglu_ref, wgate_ref, tabs)
    q_ref[...] = q.reshape(S, nb, ATTN_DIM)
    kv_ref[...] = jnp.concatenate([k, v], axis=1).reshape(S, nb, 2 * KV_DIM)
    u_ref[...] = u.reshape(S, nb, CONV_DIM)
    gates_ref[...] = gates.reshape(S, nb, 2 * D_MODEL)


def _front_sample(x, mods, n1g, inv_lane, wqkv, wglu, wgate):
    nb, S, _ = x.shape
    full = lambda shape: pl.BlockSpec(shape, lambda i: (0,) * len(shape))
    args = (x, mods, n1g, inv_lane, wqkv, wglu, wgate)
    return pl.pallas_call(
        _front_sample_kernel,
        grid=(1,),
        in_specs=[full(a.shape) for a in args],
        out_specs=[full((S, nb, ATTN_DIM)), full((S, nb, 2 * KV_DIM)), full((S, nb, CONV_DIM)),
                   full((S, nb, 2 * D_MODEL))],
        out_shape=[
            jax.ShapeDtypeStruct((S, nb, ATTN_DIM), _BF16),
            jax.ShapeDtypeStruct((S, nb, 2 * KV_DIM), _F32),
            jax.ShapeDtypeStruct((S, nb, CONV_DIM), _F32),
            jax.ShapeDtypeStruct((S, nb, 2 * D_MODEL), _BF16),
        ],
        compiler_params=pltpu.CompilerParams(
            dimension_semantics=("arbitrary",), vmem_limit_bytes=VMEM_LIMIT),
        name="front_sample",
    )(*args)


def _attn_sample_kernel(sinks_ref, q_ref, kvn_ref, kbuf_ref, vbuf_ref, a_ref, kout_ref, vout_ref):
    S = q_ref.shape[0]
    bc = q_ref.shape[1]
    L = kbuf_ref.shape[1]
    npad = 8
    rows = lax.broadcasted_iota(jnp.int32, (GROUP * S, L + npad), 0) % S
    cols = lax.broadcasted_iota(jnp.int32, (GROUP * S, L + npad), 1)
    mask = (cols > rows) & (cols <= rows + WINDOW) & (cols < L + S)
    slot_col = lax.broadcasted_iota(jnp.int32, (S, L + npad), 1) == 0
    fills = []
    for g in range(N_KV_HEADS):
        fills.append(jnp.concatenate(
            [jnp.where(slot_col, sinks_ref[g * GROUP + h], NEG_INF) for h in range(GROUP)], axis=0))
    slot_row = lax.broadcasted_iota(jnp.int32, (L + npad, HEAD_DIM), 0) == 0
    zpad = jnp.zeros((npad - S, 2 * KV_DIM), _F32)
    scores, values = [], []
    for b in range(bc):
        kvn = jnp.concatenate([kvn_ref[:, b, :], zpad], axis=0)
        kold = kbuf_ref[b]
        vold = vbuf_ref[b]
        kc = jnp.concatenate([kold, kvn[:, :KV_DIM]], axis=0)
        vc = jnp.concatenate([vold, kvn[:, KV_DIM:]], axis=0)
        kout_ref[b] = kc[S:S + L]
        vout_ref[b] = vc[S:S + L]
        kcb = kc.astype(_BF16)
        vcb = vc.astype(_BF16)
        qb = q_ref[:, b, :]
        for g in range(N_KV_HEADS):
            qg = jnp.concatenate(
                [qb[:, (g * GROUP + h) * HEAD_DIM:(g * GROUP + h + 1) * HEAD_DIM] for h in range(GROUP)],
                axis=0)
            kg = kcb[:, g * HEAD_DIM:(g + 1) * HEAD_DIM]
            vg = vcb[:, g * HEAD_DIM:(g + 1) * HEAD_DIM]
            scores.append(_dot_nt(qg, kg))
            values.append(jnp.where(slot_row, jnp.zeros_like(vg), vg))
    probs = [_softmax_sink_slot(s, mask, fills[n % N_KV_HEADS]) for n, s in enumerate(scores)]
    outs = [_dot(p, vg) / den for (p, den), vg in zip(probs, values)]
    for b in range(bc):
        heads = []
        for g in range(N_KV_HEADS):
            o = outs[b * N_KV_HEADS + g]
            heads += [o[h * S:(h + 1) * S] for h in range(GROUP)]
        a_ref[:, b, :] = jnp.concatenate(heads, axis=1).astype(_BF16)


def _attn_sample(sinks, q, kvn, kbuf, vbuf, bc):
    S, nb, _ = q.shape
    L = kbuf.shape[1]
    stepb = lambda w: pl.BlockSpec((S, bc, w), lambda i: (0, i, 0))
    buf = pl.BlockSpec((bc, L, KV_DIM), lambda i: (i, 0, 0))
    return pl.pallas_call(
        _attn_sample_kernel,
        grid=(nb // bc,),
        in_specs=[pl.BlockSpec(memory_space=pltpu.SMEM), stepb(ATTN_DIM), stepb(2 * KV_DIM), buf, buf],
        out_specs=[stepb(ATTN_DIM), buf, buf],
        out_shape=[
            jax.ShapeDtypeStruct((S, nb, ATTN_DIM), _BF16),
            jax.ShapeDtypeStruct((nb, L, KV_DIM), _F32),
            jax.ShapeDtypeStruct((nb, L, KV_DIM), _F32),
        ],
        compiler_params=pltpu.CompilerParams(dimension_semantics=("arbitrary",)),
        name="attn_sample",
    )(sinks, q, kvn, kbuf, vbuf)


def _back_sample_kernel(x_ref, a_ref, u_ref, gates_ref, mods_ref, hist_ref,
                        cw_ref, cb_ref, lng_ref, lnb_ref, n2g_ref, fng_ref,
                        wao_ref, wco_ref, wout_ref, wfi_ref, wfo_ref,
                        y_ref, hout_ref):
    nb = x_ref.shape[0]
    S = x_ref.shape[1]
    slab = lambda j: hist_ref[:, j, :] if j < CONV_HIST else u_ref[j - CONV_HIST]
    outs = []
    for s in range(S):
        dw = jnp.broadcast_to(cb_ref[...], (nb, CONV_DIM))
        for k in range(CONV_WIDTH):
            dw = dw + slab(s + k) * cw_ref[k:k + 1, :]
        outs.append(dw)
    for j in range(CONV_HIST):
        hout_ref[:, j, :] = slab(j + S)
    s_act = _conv_post(jnp.concatenate(outs, axis=0), lng_ref[...], lnb_ref[...])

    md = lambda i: jnp.concatenate([mods_ref[:, i, :]] * S, axis=0)
    x = jnp.concatenate([x_ref[:, s, :] for s in range(S)], axis=0)
    x1, h2 = _merge_body(x, a_ref[...].reshape(S * nb, ATTN_DIM), s_act,
                         gates_ref[...].reshape(S * nb, 2 * D_MODEL), md(2), md(3), md(4),
                         n2g_ref[...], wao_ref, wco_ref, wout_ref)
    y = _ffn_body(x1, h2, md(5), fng_ref[...], wfi_ref, wfo_ref)
    for s in range(S):
        y_ref[:, s, :] = y[s * nb:(s + 1) * nb]


def _back_sample(x, a, u, gates, mods, hist, cw, cb, lng, lnb, n2g, fng, wao, wco, wout, wfi, wfo):
    nb, S, _ = x.shape
    full = lambda shape: pl.BlockSpec(shape, lambda i: (0,) * len(shape))
    args = (x, a, u, gates, mods, hist, cw, cb, lng, lnb, n2g, fng, wao, wco, wout, wfi, wfo)
    return pl.pallas_call(
        _back_sample_kernel,
        grid=(1,),
        in_specs=[full(v.shape) for v in args],
        out_specs=[full((nb, S, D_MODEL)), full((nb, CONV_HIST, CONV_DIM))],
        out_shape=[
            jax.ShapeDtypeStruct((nb, S, D_MODEL), _F32),
            jax.ShapeDtypeStruct((nb, CONV_HIST, CONV_DIM), _F32),
        ],
        compiler_params=pltpu.CompilerParams(
            dimension_semantics=("arbitrary",), vmem_limit_bytes=VMEM_LIMIT),
        name="back_sample",
    )(*args)


def _inv_lane():
    inv = ROPE_THETA ** (-jnp.arange(0, ROPE_DIM, 2, dtype=_F32) / ROPE_DIM)
    return jnp.tile(inv, LANES // ROPE_HALF).reshape(1, LANES)


def _layer(xp, xs, kbuf, vbuf, cbuf, mods_p, mods_s, lw, final_g, tt_front, tt_back, bc):
    row = lambda v: v.reshape(1, -1)
    w_in = lw['w_in'].astype(_BF16)
    wqkv = w_in[:, :QKV_COLS]
    wglu = w_in[:, QKV_COLS:QKV_COLS + 2 * CONV_DIM]
    wgate = w_in[:, QKV_COLS + 2 * CONV_DIM:]
    wao = lw['w_attn_o'].astype(_BF16)
    wco = lw['w_conv_o'].astype(_BF16)
    wout = lw['w_out'].astype(_BF16)
    wfi = lw['w_ffn_in'].astype(_BF16)
    wfo = lw['w_ffn_out'].astype(_BF16)
    inv_lane = _inv_lane()
    n1g, n2g, fng = row(lw['norm1_g']), row(lw['norm2_g']), row(final_g)
    cw, cb, lng, lnb = lw['conv_w'], row(lw['conv_b']), row(lw['conv_ln_g']), row(lw['conv_ln_b'])
    sinks = lw['sinks']

    qkv, u, gates, kvlast = _front_prompt(xp, mods_p, n1g, inv_lane, wqkv, wglu, wgate, tt_front)
    yp = _back_prompt(sinks, xp, qkv, u, gates, mods_p, cw, cb, lng, lnb, n2g, fng,
                      wao, wco, wout, wfi, wfo, tt_back)

    q_s, kvn_s, u_s, gates_s = _front_sample(xs, mods_s, n1g, inv_lane, wqkv, wglu, wgate)
    a_s, kout, vout = _attn_sample(sinks, q_s, kvn_s, kbuf, vbuf, bc)
    ys, hout = _back_sample(xs, a_s, u_s, gates_s, mods_s, cbuf, cw, cb, lng, lnb, n2g, fng,
                            wao, wco, wout, wfi, wfo)
    return yp, ys, kvlast, u, kout, vout, hout


def kernel(x_prompt, x_sample, state_k_win, state_v_win, state_conv, c_prompt, c_sample, norm1_g, norm2_g,
           w_ada, b_ada, w_in, sinks, w_attn_o, conv_w, conv_b, conv_ln_g, conv_ln_b, w_conv_o, w_out,
           w_ffn_in, w_ffn_out, final_norm_g):
    B, T, _ = x_prompt.shape
    nb, S, _ = x_sample.shape
    depth = w_in.shape[0]
    assert depth == 1, "final norm is fused into the single trunk layer"
    L = state_k_win.shape[2]
    l = 0
    lw = dict(norm1_g=norm1_g[l], norm2_g=norm2_g[l], w_in=w_in[l], sinks=sinks[l], w_attn_o=w_attn_o[l],
              conv_w=conv_w[l], conv_b=conv_b[l], conv_ln_g=conv_ln_g[l], conv_ln_b=conv_ln_b[l],
              w_conv_o=w_conv_o[l], w_out=w_out[l], w_ffn_in=w_ffn_in[l], w_ffn_out=w_ffn_out[l])

    mods = _adaln_mods(jnp.concatenate([c_prompt, c_sample], axis=0), w_ada[l], b_ada[l])
    mods = mods.reshape(B + nb, 6, D_MODEL)
    mods_p = mods[:B]
    mods_s = mods[B:]
    kbuf = state_k_win[l].reshape(nb, L, KV_DIM)
    vbuf = state_v_win[l].reshape(nb, L, KV_DIM)

    tt_front = min(512, T)
    tt_back = min(256, T)
    yp, y_sample, kvtiles, u, kout, vout, hout = _layer(
        x_prompt, x_sample, kbuf, vbuf, state_conv[l], mods_p, mods_s, lw, final_norm_g, tt_front, tt_back, bc=8)

    kvlast = kvtiles[-1]
    k_win_prompt = kvlast[:, :, :KV_DIM].reshape(1, B, WINDOW, N_KV_HEADS, HEAD_DIM)
    v_win_prompt = kvlast[:, :, KV_DIM:].reshape(1, B, WINDOW, N_KV_HEADS, HEAD_DIM)
    conv_prompt = u[:, T - CONV_HIST:, :][None]
    k_win_sample = kout.reshape(1, nb, L, N_KV_HEADS, HEAD_DIM)
    v_win_sample = vout.reshape(1, nb, L, N_KV_HEADS, HEAD_DIM)
    conv_sample = hout[None]
    return (yp, y_sample, k_win_prompt, v_win_prompt, conv_prompt, k_win_sample, v_win_sample, conv_sample)
```

```python
import functools

import jax
import jax.numpy as jnp
import numpy as np
from jax import lax
from jax.experimental import pallas as pl
from jax.experimental.pallas import tpu as pltpu

D_MODEL = 1024
HEAD_DIM = 64
N_HEADS = 8
N_KV_HEADS = 2
GROUP = N_HEADS // N_KV_HEADS
ATTN_DIM = N_HEADS * HEAD_DIM
KV_DIM = N_KV_HEADS * HEAD_DIM
WINDOW = 128
ROPE_THETA = 500000.0
ROPE_DIM = HEAD_DIM // 4
ROPE_HALF = ROPE_DIM // 2
CONV_DIM = D_MODEL // 2
CONV_WIDTH = 31
CONV_HIST = CONV_WIDTH - 1
FFN_DIM = -(-8 * D_MODEL // (3 * 256)) * 256
PAST_LEN = 16384
RMS_EPS = 1e-6
LN_EPS = 1e-5
NEG_INF = -1e30

LANES = 128
SUBLANES = 8
HIST_PAD = 32
QK_COLS = ATTN_DIM + KV_DIM
QKV_COLS = ATTN_DIM + 2 * KV_DIM
VMEM_LIMIT = 56 * 1024 * 1024

_F32 = jnp.float32
_BF16 = jnp.bfloat16


def _dot(a, b):
    return jnp.dot(a, b, preferred_element_type=_F32)


def _dot_nt(a, b):
    return lax.dot_general(a, b, (((1,), (1,)), ((), ())), preferred_element_type=_F32)


def _rms(x, g):
    return x * lax.rsqrt(jnp.mean(x * x, axis=-1, keepdims=True) + RMS_EPS) * g


def _sigmoid(x):
    return 1.0 / (1.0 + jnp.exp(-x))


def _const_spec(shape):
    nd = len(shape)
    return pl.BlockSpec(shape, lambda *_: (0,) * nd, pipeline_mode=pl.Buffered(1))


def _mod(mods_ref, rows, k):
    return mods_ref[rows, k * D_MODEL:(k + 1) * D_MODEL]


W_QKV = slice(0, QKV_COLS)
W_GLU = slice(QKV_COLS, QKV_COLS + 2 * CONV_DIM)
W_GATE = slice(QKV_COLS + 2 * CONV_DIM, QKV_COLS + 2 * CONV_DIM + 2 * D_MODEL)


def _mods_kernel(c_ref, w_ref, b_ref, o_ref):
    c = c_ref[...]
    a = (c * _sigmoid(c)).astype(_BF16)
    o_ref[...] = _dot(a, w_ref[...].astype(_BF16)) + b_ref[...]


def _adaln_mods(c, w_ada, b_ada):
    n = c.shape[0]
    cols = w_ada.shape[1]
    bn = D_MODEL
    return pl.pallas_call(
        _mods_kernel,
        grid=(cols // bn,),
        in_specs=[
            pl.BlockSpec((n, D_MODEL), lambda j: (0, 0)),
            pl.BlockSpec((D_MODEL, bn), lambda j: (0, j)),
            pl.BlockSpec((1, bn), lambda j: (0, j)),
        ],
        out_specs=pl.BlockSpec((n, bn), lambda j: (0, j)),
        out_shape=jax.ShapeDtypeStruct((n, cols), _F32),
        compiler_params=pltpu.CompilerParams(dimension_semantics=("arbitrary",)),
        name="adaln_mods",
    )(c, w_ada, b_ada.reshape(1, cols))


def _rope_tables(pos, inv_lane):
    ang = pos * inv_lane
    cos = jnp.cos(ang)
    sin = jnp.sin(ang)
    lane = lax.broadcasted_iota(jnp.int32, (1, LANES), 1) % HEAD_DIM
    c = jnp.where(lane < ROPE_DIM, cos, 1.0)
    s_up = jnp.where((lane >= ROPE_HALF) & (lane < ROPE_DIM), sin, 0.0)
    s_dn = jnp.where(lane < ROPE_HALF, -sin, 0.0)
    return c, s_up, s_dn


def _apply_rope(z, c, s_up, s_dn):
    outs = []
    for j in range(z.shape[1] // LANES):
        zz = z[:, j * LANES:(j + 1) * LANES]
        outs.append(zz * c + pltpu.roll(zz, ROPE_HALF, 1) * s_up
                    + pltpu.roll(zz, LANES - ROPE_HALF, 1) * s_dn)
    return jnp.concatenate(outs, axis=1)


def _front_body(x, sh, sc, n1g, win_ref, tabs):
    h = (_rms(x, n1g) * (1.0 + sc) + sh).astype(_BF16)
    zqkv = _dot(h, win_ref[:, W_QKV])
    qk = _apply_rope(zqkv[:, :QK_COLS], *tabs)
    q = (qk[:, :ATTN_DIM] * (HEAD_DIM ** -0.5)).astype(_BF16)
    k = qk[:, ATTN_DIM:]
    v = zqkv[:, QK_COLS:]
    zg = _dot(h, win_ref[:, W_GLU])
    u = zg[:, :CONV_DIM] * _sigmoid(zg[:, CONV_DIM:])
    gates = _sigmoid(_dot(h, win_ref[:, W_GATE])).astype(_BF16)
    return q, k, v, u, gates


def _front_prompt_kernel(x_ref, mods_ref, n1g_ref, inv_ref, win_ref,
                         qkv_ref, u_ref, gates_ref, kvlast_ref, tab_ref):
    t = pl.program_id(0)
    b = pl.program_id(1)
    tt = x_ref.shape[1]

    @pl.when(b == 0)
    def _():
        pos = (t * tt + lax.broadcasted_iota(jnp.int32, (tt, 1), 0)).astype(_F32)
        c, s_up, s_dn = _rope_tables(pos, inv_ref[...])
        tab_ref[0] = c
        tab_ref[1] = s_up
        tab_ref[2] = s_dn

    sh = _mod(mods_ref, pl.ds(b, 1), 0)
    sc = _mod(mods_ref, pl.ds(b, 1), 1)
    half = tt // 2
    for r0 in (0, half):
        rows = slice(r0, r0 + half)
        q, k, v, u, gates = _front_body(x_ref[0, rows, :], sh, sc, n1g_ref[...], win_ref,
                                        (tab_ref[0, rows, :], tab_ref[1, rows, :], tab_ref[2, rows, :]))
        qkv_ref[0, rows, :] = jnp.concatenate([q, k.astype(_BF16), v.astype(_BF16)], axis=1)
        u_ref[0, rows, :] = u
        gates_ref[0, rows, :] = gates
    kvlast_ref[0, 0] = jnp.concatenate([k[half - WINDOW:], v[half - WINDOW:]], axis=1)


def _front_prompt(x, mods, n1g, inv_lane, win, tt):
    B, T, _ = x.shape
    grid = (T // tt, B)
    tok = lambda w: pl.BlockSpec((1, tt, w), lambda t, b: (b, t, 0))
    return pl.pallas_call(
        _front_prompt_kernel,
        grid=grid,
        in_specs=[
            tok(D_MODEL),
            _const_spec(mods.shape),
            _const_spec((1, D_MODEL)),
            _const_spec((1, LANES)),
            _const_spec(win.shape),
        ],
        out_specs=[
            tok(QKV_COLS),
            tok(CONV_DIM),
            tok(2 * D_MODEL),
            pl.BlockSpec((1, 1, WINDOW, 2 * KV_DIM), lambda t, b: (t, b, 0, 0)),
        ],
        out_shape=[
            jax.ShapeDtypeStruct((B, T, QKV_COLS), _BF16),
            jax.ShapeDtypeStruct((B, T, CONV_DIM), _F32),
            jax.ShapeDtypeStruct((B, T, 2 * D_MODEL), _BF16),
            jax.ShapeDtypeStruct((T // tt, B, WINDOW, 2 * KV_DIM), _F32),
        ],
        scratch_shapes=[pltpu.VMEM((3, tt, LANES), _F32)],
        compiler_params=pltpu.CompilerParams(
            dimension_semantics=("arbitrary", "arbitrary"), vmem_limit_bytes=VMEM_LIMIT),
        name="front_prompt",
    )(x, mods, n1g, inv_lane, win)


def _order_after(x, anchor):
    bits = pltpu.bitcast(anchor[0:SUBLANES, 0:x.shape[1]], jnp.uint32)
    zero = pltpu.bitcast(lax.shift_right_logical(bits, jnp.uint32(32)), _F32)
    return jnp.concatenate([x[0:SUBLANES] + zero, x[SUBLANES:]], axis=0)


def _conv_post(dw, lng, lnb, anchor=None):
    mu = jnp.mean(dw, axis=-1, keepdims=True)
    d = dw - mu
    var = jnp.mean(d * d, axis=-1, keepdims=True)
    n = d * lax.rsqrt(var + LN_EPS) * lng + lnb
    s = n * _sigmoid(n)
    if anchor is not None:
        s = _order_after(s, anchor)
    return s.astype(_BF16)


def _ffn_chunks():
    step = 4 * 256
    return [(s, min(step, FFN_DIM - s)) for s in range(0, FFN_DIM, step)]


def _merge_body(x, a, s, gates, g1, sh2, sc2, n2g, wao_ref, wco_ref, wout_ref):
    a_out = _dot(a, wao_ref[...])
    cv_out = _dot(s, wco_ref[...])
    m = (gates[:, :D_MODEL].astype(_F32) * a_out + gates[:, D_MODEL:].astype(_F32) * cv_out).astype(_BF16)
    x1 = x + g1 * _dot(m, wout_ref[...])
    h2 = (_rms(x1, n2g) * (1.0 + sc2) + sh2).astype(_BF16)
    return x1, h2


def _ffn_body(x1, h2, g2, fng, wfi_ref, wfo_ref):
    acc = None
    for (c0, cw) in _ffn_chunks():
        g = _dot(h2, wfi_ref[:, c0:c0 + cw])
        uu = _dot(h2, wfi_ref[:, FFN_DIM + c0:FFN_DIM + c0 + cw])
        act = (g * _sigmoid(g) * uu).astype(_BF16)
        part = _dot(act, wfo_ref[c0:c0 + cw, :])
        acc = part if acc is None else acc + part
    x2 = x1 + g2 * acc
    return _rms(x2, fng)


def _softmax_sink_slot(s, mask, fill):
    s = jnp.where(mask, s, fill)
    p = jnp.exp(s - jnp.max(s, axis=-1, keepdims=True))
    return p.astype(_BF16), jnp.sum(p, axis=-1, keepdims=True)


CONV_CHUNK = 4 * SUBLANES


def _causal_conv(hist_ref, cw_ref, cb_ref, row0, row1):
    off = HIST_PAD - CONV_HIST
    chunk = CONV_CHUNK
    outs = []
    for c0 in range(row0, row1, chunk):
        acc_out = jnp.broadcast_to(cb_ref[...], (chunk, CONV_DIM))
        for r in range(SUBLANES):
            rows = chunk + (SUBLANES if r else 0)
            acc = None
            for a in range((off + CONV_WIDTH - 1 - r) // SUBLANES + 1):
                k = SUBLANES * a + r - off
                if k < 0:
                    continue
                r_lo = c0 + SUBLANES * a
                term = hist_ref[r_lo:r_lo + rows, :] * cw_ref[k:k + 1, :]
                acc = term if acc is None else acc + term
            acc_out = acc_out + acc[r:r + chunk]
        outs.append(acc_out)
    return outs


def _back_prompt_kernel(nt, sinks_ref, x_ref, qkv_ref, u_ref, gates_ref, mods_ref,
                        cw_ref, cb_ref, lng_ref, lnb_ref, n2g_ref, fng_ref,
                        wao_ref, wco_ref, wout_ref, wfi_ref, wfo_ref,
                        y_ref, kvc_ref, hist_ref, x1_ref, h2_ref):
    i = pl.program_id(0)
    tile_a = jnp.minimum(i, pl.num_programs(0) - 2)
    t = tile_a % nt
    b_a = tile_a // nt
    b_b = jnp.maximum(i - 1, 0) // nt
    tt = x_ref.shape[1]
    nblk = tt // WINDOW

    @pl.when(i == 0)
    def _():
        x1_ref[...] = jnp.zeros_like(x1_ref)
        h2_ref[...] = jnp.zeros_like(h2_ref)

    @pl.when(t == 0)
    def _():
        kvc_ref[...] = jnp.zeros_like(kvc_ref)
        hist_ref[0:HIST_PAD, :] = jnp.zeros((HIST_PAD, CONV_DIM), _F32)

    md_a = lambda k: _mod(mods_ref, pl.ds(b_a, 1), k)
    md_b = lambda k: _mod(mods_ref, pl.ds(b_b, 1), k)
    x1_b = x1_ref[...]
    h2_b = h2_ref[...]
    chunks = _ffn_chunks()
    ffn_up = lambda c: (_dot(h2_b, wfi_ref[:, c[0]:c[0] + c[1]]),
                        _dot(h2_b, wfi_ref[:, FFN_DIM + c[0]:FFN_DIM + c[0] + c[1]]))
    ffn_down = lambda c, gu: _dot((gu[0] * _sigmoid(gu[0]) * gu[1]).astype(_BF16), wfo_ref[c[0]:c[0] + c[1], :])

    rows = lax.broadcasted_iota(jnp.int32, (GROUP * WINDOW, 2 * WINDOW), 0) % WINDOW
    cols = lax.broadcasted_iota(jnp.int32, (GROUP * WINDOW, 2 * WINDOW), 1)
    band = (cols > rows) & (cols <= rows + WINDOW)
    first_lo = WINDOW * (1 - jnp.minimum(t, 1))
    band_first = band & (cols >= first_lo)
    slot_col = lax.broadcasted_iota(jnp.int32, (WINDOW, 2 * WINDOW), 1) == 0
    fills = []
    for g in range(N_KV_HEADS):
        fills.append(jnp.concatenate(
            [jnp.where(slot_col, sinks_ref[g * GROUP + h], NEG_INF) for h in range(GROUP)], axis=0))
    slot_row = lax.broadcasted_iota(jnp.int32, (2 * WINDOW, HEAD_DIM), 0) == 0
    kv_prev = kvc_ref[...].astype(_BF16)
    units = []
    for j in range(nblk):
        r0 = j * WINDOW
        qb = qkv_ref[0, r0:r0 + WINDOW, 0:ATTN_DIM]
        kv_cur = qkv_ref[0, r0:r0 + WINDOW, ATTN_DIM:QKV_COLS]
        kv = jnp.concatenate([kv_prev, kv_cur], axis=0)
        for g in range(N_KV_HEADS):
            qg = jnp.concatenate(
                [qb[:, (g * GROUP + h) * HEAD_DIM:(g * GROUP + h + 1) * HEAD_DIM] for h in range(GROUP)],
                axis=0)
            kg = kv[:, g * HEAD_DIM:(g + 1) * HEAD_DIM]
            vg = kv[:, KV_DIM + g * HEAD_DIM:KV_DIM + (g + 1) * HEAD_DIM]
            vg = jnp.where(slot_row, jnp.zeros_like(vg), vg)
            units.append((_dot_nt(qg, kg), band_first if j == 0 else band, fills[g], vg))
        kv_prev = kv_cur
    kvc_ref[...] = kv_prev.astype(_F32)

    hist_ref[HIST_PAD:HIST_PAD + tt, :] = u_ref[0]
    half = tt // 2
    conv_branch = lambda r0, r1, anchor: _dot(
        _conv_post(jnp.concatenate(_causal_conv(hist_ref, cw_ref, cb_ref, r0, r1), axis=0),
                   lng_ref[...], lnb_ref[...], anchor), wco_ref[...])

    gu0 = ffn_up(chunks[0])
    outs = []
    for (s, mask, fill, vg) in units:
        p, den = _softmax_sink_slot(s, mask, fill)
        outs.append(_dot(p, vg) / den)
    gu1 = ffn_up(chunks[1])
    cv_lo = conv_branch(0, half, gu1[1])
    down0 = ffn_down(chunks[0], gu0)
    acc = down0

    a_blocks = []
    for j in range(nblk):
        heads = []
        for g in range(N_KV_HEADS):
            o = outs[j * N_KV_HEADS + g]
            heads += [o[h * WINDOW:(h + 1) * WINDOW] for h in range(GROUP)]
        a_blocks.append(jnp.concatenate(heads, axis=1))
    a = jnp.concatenate(a_blocks, axis=0).astype(_BF16)

    a_out = _dot(a, wao_ref[...])
    cv_out = jnp.concatenate([cv_lo, conv_branch(half, tt, down0)], axis=0)
    hist_ref[0:HIST_PAD, :] = hist_ref[tt:tt + HIST_PAD, :]
    acc = acc + ffn_down(chunks[1], gu1)
    gu2 = ffn_up(chunks[2])
    gates = gates_ref[0]
    m = (gates[:, :D_MODEL].astype(_F32) * a_out + gates[:, D_MODEL:].astype(_F32) * cv_out).astype(_BF16)
    x1 = x_ref[0] + md_a(2) * _dot(m, wout_ref[...])
    acc = acc + ffn_down(chunks[2], gu2)
    h2 = (_rms(x1, n2g_ref[...]) * (1.0 + md_a(4)) + md_a(3)).astype(_BF16)

    y_ref[0] = _rms(x1_b + md_b(5) * acc, fng_ref[...])
    x1_ref[...] = x1
    h2_ref[...] = h2


def _back_prompt(sinks, x, qkv, u, gates, mods, cw, cb, lng, lnb, n2g, fng, wao, wco, wout, wfi, wfo, tt):
    B, T, _ = x.shape
    nt = T // tt
    n_tiles = B * nt
    tile_in = lambda i: jnp.minimum(i, n_tiles - 1)
    tile_out = lambda i: jnp.maximum(i - 1, 0)
    tok_in = lambda w: pl.BlockSpec((1, tt, w), lambda i: (tile_in(i) // nt, tile_in(i) % nt, 0))
    consts = [cw, cb, lng, lnb, n2g, fng, wao, wco, wout, wfi, wfo]
    return pl.pallas_call(
        functools.partial(_back_prompt_kernel, nt),
        grid=(n_tiles + 1,),
        in_specs=[
            pl.BlockSpec(memory_space=pltpu.SMEM),
            tok_in(D_MODEL), tok_in(QKV_COLS), tok_in(CONV_DIM), tok_in(2 * D_MODEL),
            _const_spec(mods.shape),
        ] + [_const_spec(c.shape) for c in consts],
        out_specs=pl.BlockSpec((1, tt, D_MODEL), lambda i: (tile_out(i) // nt, tile_out(i) % nt, 0)),
        out_shape=jax.ShapeDtypeStruct((B, T, D_MODEL), _F32),
        scratch_shapes=[
            pltpu.VMEM((WINDOW, 2 * KV_DIM), _F32),
            pltpu.VMEM((HIST_PAD + tt, CONV_DIM), _F32),
            pltpu.VMEM((tt, D_MODEL), _F32),
            pltpu.VMEM((tt, D_MODEL), _BF16),
        ],
        compiler_params=pltpu.CompilerParams(
            dimension_semantics=("arbitrary",), vmem_limit_bytes=VMEM_LIMIT),
        name="back_prompt",
    )(sinks, x, qkv, u, gates, mods, *consts)


def _front_sample_kernel(x_ref, mods_ref, n1g_ref, inv_ref, win_ref,
                         q_ref, kv_ref, u_ref, gates_ref):
    nb = x_ref.shape[0]
    S = x_ref.shape[1]
    mrows = slice(mods_ref.shape[0] - nb, mods_ref.shape[0])
    pos = (PAST_LEN + lax.broadcasted_iota(jnp.int32, (8, 1), 0)).astype(_F32)
    c8, up8, dn8 = _rope_tables(pos, inv_ref[...])
    expand = lambda tab: jnp.concatenate(
        [jnp.broadcast_to(tab[s:s + 1, :], (nb, LANES)) for s in range(S)], axis=0)
    tabs = (expand(c8), expand(up8), expand(dn8))
    x = jnp.concatenate([x_ref[:, s, :] for s in range(S)], axis=0)
    sh = jnp.concatenate([_mod(mods_ref, mrows, 0)] * S, axis=0)
    sc = jnp.concatenate([_mod(mods_ref, mrows, 1)] * S, axis=0)
    q, k, v, u, gates = _front_body(x, sh, sc, n1g_ref[...], win_ref, tabs)
    q_ref[...] = q.reshape(S, nb, ATTN_DIM)
    kv_ref[...] = jnp.concatenate([k, v], axis=1).reshape(S, nb, 2 * KV_DIM)
    u_ref[...] = u.reshape(S, nb, CONV_DIM)
    gates_ref[...] = gates.reshape(S, nb, 2 * D_MODEL)


def _front_sample(x, mods, n1g, inv_lane, win):
    nb, S, _ = x.shape
    full = lambda shape: pl.BlockSpec(shape, lambda i: (0,) * len(shape))
    args = (x, mods, n1g, inv_lane, win)
    return pl.pallas_call(
        _front_sample_kernel,
        grid=(1,),
        in_specs=[full(a.shape) for a in args],
        out_specs=[full((S, nb, ATTN_DIM)), full((S, nb, 2 * KV_DIM)), full((S, nb, CONV_DIM)),
                   full((S, nb, 2 * D_MODEL))],
        out_shape=[
            jax.ShapeDtypeStruct((S, nb, ATTN_DIM), _BF16),
            jax.ShapeDtypeStruct((S, nb, 2 * KV_DIM), _F32),
            jax.ShapeDtypeStruct((S, nb, CONV_DIM), _F32),
            jax.ShapeDtypeStruct((S, nb, 2 * D_MODEL), _BF16),
        ],
        compiler_params=pltpu.CompilerParams(
            dimension_semantics=("arbitrary",), vmem_limit_bytes=VMEM_LIMIT),
        name="front_sample",
    )(*args)


def _attn_sample_kernel(sinks_ref, q_ref, kvn_ref, kbuf_ref, vbuf_ref, a_ref, kout_ref, vout_ref):
    S = q_ref.shape[0]
    bc = q_ref.shape[1]
    L = kbuf_ref.shape[2]
    npad = 8
    rows = lax.broadcasted_iota(jnp.int32, (GROUP * S, L + npad), 0) % S
    cols = lax.broadcasted_iota(jnp.int32, (GROUP * S, L + npad), 1)
    mask = (cols > rows) & (cols <= rows + WINDOW) & (cols < L + S)
    slot_col = lax.broadcasted_iota(jnp.int32, (S, L + npad), 1) == 0
    fills = []
    for g in range(N_KV_HEADS):
        fills.append(jnp.concatenate(
            [jnp.where(slot_col, sinks_ref[g * GROUP + h], NEG_INF) for h in range(GROUP)], axis=0))
    slot_lane = lax.broadcasted_iota(jnp.int32, (HEAD_DIM, L + npad), 1) == 0
    zpad = jnp.zeros((npad - S, 2 * KV_DIM), _F32)
    scores, values = [], []
    for b in range(bc):
        kvn = jnp.concatenate([kvn_ref[:, b, :], zpad], axis=0)
        kc = jnp.concatenate([kbuf_ref[b], kvn[:, :KV_DIM].T], axis=1)
        vc = jnp.concatenate([vbuf_ref[b], kvn[:, KV_DIM:].T], axis=1)
        kout_ref[b] = kc[:, S:S + L]
        vout_ref[b] = vc[:, S:S + L]
        kcb = kc.astype(_BF16)
        vcb = vc.astype(_BF16)
        qb = q_ref[:, b, :]
        for g in range(N_KV_HEADS):
            qg = jnp.concatenate(
                [qb[:, (g * GROUP + h) * HEAD_DIM:(g * GROUP + h + 1) * HEAD_DIM] for h in range(GROUP)],
                axis=0)
            kg = kcb[g * HEAD_DIM:(g + 1) * HEAD_DIM, :]
            vg = vcb[g * HEAD_DIM:(g + 1) * HEAD_DIM, :]
            scores.append(_dot(qg, kg))
            values.append(jnp.where(slot_lane, jnp.zeros_like(vg), vg))
    probs = [_softmax_sink_slot(s, mask, fills[n % N_KV_HEADS]) for n, s in enumerate(scores)]
    outs = [_dot_nt(p, vg) / den for (p, den), vg in zip(probs, values)]
    for b in range(bc):
        heads = []
        for g in range(N_KV_HEADS):
            o = outs[b * N_KV_HEADS + g]
            heads += [o[h * S:(h + 1) * S] for h in range(GROUP)]
        a_ref[:, b, :] = jnp.concatenate(heads, axis=1).astype(_BF16)


def _attn_sample(sinks, q, kvn, kbuf, vbuf, bc):
    S, nb, _ = q.shape
    L = kbuf.shape[2]
    stepb = lambda w: pl.BlockSpec((S, bc, w), lambda i: (0, i, 0))
    buf = pl.BlockSpec((bc, KV_DIM, L), lambda i: (i, 0, 0))
    return pl.pallas_call(
        _attn_sample_kernel,
        grid=(nb // bc,),
        in_specs=[pl.BlockSpec(memory_space=pltpu.SMEM), stepb(ATTN_DIM), stepb(2 * KV_DIM), buf, buf],
        out_specs=[stepb(ATTN_DIM), buf, buf],
        out_shape=[
            jax.ShapeDtypeStruct((S, nb, ATTN_DIM), _BF16),
            jax.ShapeDtypeStruct((nb, KV_DIM, L), _F32),
            jax.ShapeDtypeStruct((nb, KV_DIM, L), _F32),
        ],
        compiler_params=pltpu.CompilerParams(dimension_semantics=("arbitrary",)),
        name="attn_sample",
    )(sinks, q, kvn, kbuf, vbuf)


def _back_sample_kernel(x_ref, a_ref, u_ref, gates_ref, mods_ref, hist_ref,
                        cw_ref, cb_ref, lng_ref, lnb_ref, n2g_ref, fng_ref,
                        wao_ref, wco_ref, wout_ref, wfi_ref, wfo_ref,
                        y_ref, hout_ref):
    nb = x_ref.shape[0]
    S = x_ref.shape[1]
    slab = lambda j: hist_ref[j] if j < CONV_HIST else u_ref[j - CONV_HIST]
    outs = []
    for s in range(S):
        dw = jnp.broadcast_to(cb_ref[...], (nb, CONV_DIM))
        for k in range(CONV_WIDTH):
            dw = dw + slab(s + k) * cw_ref[k:k + 1, :]
        outs.append(dw)
    for j in range(CONV_HIST):
        hout_ref[j] = slab(j + S)
    s_act = _conv_post(jnp.concatenate(outs, axis=0), lng_ref[...], lnb_ref[...])

    mrows = slice(mods_ref.shape[0] - nb, mods_ref.shape[0])
    md = lambda i: jnp.concatenate([_mod(mods_ref, mrows, i)] * S, axis=0)
    x = jnp.concatenate([x_ref[:, s, :] for s in range(S)], axis=0)
    x1, h2 = _merge_body(x, a_ref[...].reshape(S * nb, ATTN_DIM), s_act,
                         gates_ref[...].reshape(S * nb, 2 * D_MODEL), md(2), md(3), md(4),
                         n2g_ref[...], wao_ref, wco_ref, wout_ref)
    y = _ffn_body(x1, h2, md(5), fng_ref[...], wfi_ref, wfo_ref)
    for s in range(S):
        y_ref[:, s, :] = y[s * nb:(s + 1) * nb]


def _back_sample(x, a, u, gates, mods, hist, cw, cb, lng, lnb, n2g, fng, wao, wco, wout, wfi, wfo):
    nb, S, _ = x.shape
    full = lambda shape: pl.BlockSpec(shape, lambda i: (0,) * len(shape))
    args = (x, a, u, gates, mods, hist, cw, cb, lng, lnb, n2g, fng, wao, wco, wout, wfi, wfo)
    return pl.pallas_call(
        _back_sample_kernel,
        grid=(1,),
        in_specs=[full(v.shape) for v in args],
        out_specs=[full((nb, S, D_MODEL)), full((CONV_HIST, nb, CONV_DIM))],
        out_shape=[
            jax.ShapeDtypeStruct((nb, S, D_MODEL), _F32),
            jax.ShapeDtypeStruct((CONV_HIST, nb, CONV_DIM), _F32),
        ],
        compiler_params=pltpu.CompilerParams(
            dimension_semantics=("arbitrary",), vmem_limit_bytes=VMEM_LIMIT),
        name="back_sample",
    )(*args)


def _inv_lane():
    inv = ROPE_THETA ** (-jnp.arange(0, ROPE_DIM, 2, dtype=_F32) / ROPE_DIM)
    return jnp.tile(inv, LANES // ROPE_HALF).reshape(1, LANES)


def _layer(xp, xs, kbuf, vbuf, cbuf, mods, lw, final_g, tt_front, tt_back, bc):
    row = lambda v: v.reshape(1, -1)
    win = lw['w_in'].astype(_BF16)
    wao = lw['w_attn_o'].astype(_BF16)
    wco = lw['w_conv_o'].astype(_BF16)
    wout = lw['w_out'].astype(_BF16)
    wfi = lw['w_ffn_in'].astype(_BF16)
    wfo = lw['w_ffn_out'].astype(_BF16)
    inv_lane = _inv_lane()
    n1g, n2g, fng = row(lw['norm1_g']), row(lw['norm2_g']), row(final_g)
    cw, cb, lng, lnb = lw['conv_w'], row(lw['conv_b']), row(lw['conv_ln_g']), row(lw['conv_ln_b'])
    sinks = lw['sinks']

    qkv, u, gates, kvlast = _front_prompt(xp, mods, n1g, inv_lane, win, tt_front)
    yp = _back_prompt(sinks, xp, qkv, u, gates, mods, cw, cb, lng, lnb, n2g, fng,
                      wao, wco, wout, wfi, wfo, tt_back)

    q_s, kvn_s, u_s, gates_s = _front_sample(xs, mods, n1g, inv_lane, win)
    a_s, kout, vout = _attn_sample(sinks, q_s, kvn_s, kbuf, vbuf, bc)
    ys, hout = _back_sample(xs, a_s, u_s, gates_s, mods, cbuf, cw, cb, lng, lnb, n2g, fng,
                            wao, wco, wout, wfi, wfo)
    return yp, ys, kvlast, u, kout, vout, hout


def kernel(x_prompt, x_sample, state_k_win, state_v_win, state_conv, c_prompt, c_sample, norm1_g, norm2_g,
           w_ada, b_ada, w_in, sinks, w_attn_o, conv_w, conv_b, conv_ln_g, conv_ln_b, w_conv_o, w_out,
           w_ffn_in, w_ffn_out, final_norm_g):
    B, T, _ = x_prompt.shape
    nb, S, _ = x_sample.shape
    depth = w_in.shape[0]
    assert depth == 1, "final norm is fused into the single trunk layer"
    L = state_k_win.shape[2]
    l = 0
    lw = dict(norm1_g=norm1_g[l], norm2_g=norm2_g[l], w_in=w_in[l], sinks=sinks[l], w_attn_o=w_attn_o[l],
              conv_w=conv_w[l], conv_b=conv_b[l], conv_ln_g=conv_ln_g[l], conv_ln_b=conv_ln_b[l],
              w_conv_o=w_conv_o[l], w_out=w_out[l], w_ffn_in=w_ffn_in[l], w_ffn_out=w_ffn_out[l])

    mods = _adaln_mods(jnp.concatenate([c_prompt, c_sample], axis=0), w_ada[l], b_ada[l])
    kbuf = state_k_win[l].reshape(nb, L, KV_DIM).transpose(0, 2, 1)
    vbuf = state_v_win[l].reshape(nb, L, KV_DIM).transpose(0, 2, 1)
    cbuf = state_conv[l].transpose(1, 0, 2)

    tt_front = min(512, T)
    tt_back = min(256, T)
    yp, y_sample, kvtiles, u, kout, vout, hout = _layer(
        x_prompt, x_sample, kbuf, vbuf, cbuf, mods, lw, final_norm_g, tt_front, tt_back, bc=8)

    kvlast = kvtiles[-1]
    k_win_prompt = kvlast[:, :, :KV_DIM].reshape(1, B, WINDOW, N_KV_HEADS, HEAD_DIM)
    v_win_prompt = kvlast[:, :, KV_DIM:].reshape(1, B, WINDOW, N_KV_HEADS, HEAD_DIM)
    conv_prompt = u[:, T - CONV_HIST:, :][None]
    k_win_sample = kout.transpose(0, 2, 1).reshape(1, nb, L, N_KV_HEADS, HEAD_DIM)
    v_win_sample = vout.transpose(0, 2, 1).reshape(1, nb, L, N_KV_HEADS, HEAD_DIM)
    conv_sample = hout.transpose(1, 0, 2)[None]
    return (yp, y_sample, k_win_prompt, v_win_prompt, conv_prompt, k_win_sample, v_win_sample, conv_sample)
```
